```python
import math
import jax, jax.numpy as jnp
from jax import lax
import numpy as np

D_MODEL = 1024
BATCH = 8
SEQ = 4096
DEPTH = 2

DA_HEADS = 4
DA_QK_DIM = 64
DA_V_DIM = 2 * DA_QK_DIM
RET_HEADS = 4
RET_QK_DIM = 64
RET_V_DIM = 64
RET_CHUNK = 128
SC_WIDTH = 256
SC_GROUPS = 4
CONV_W = 3
D_FF = 2816
REL_BUCKETS = 32
REL_MAX_DIST = 128
ROPE_THETA = 10000.0
Q_BLOCK = 128
EPS = 1e-6
N_BRANCH = 3

DA_QK_W = DA_HEADS * 2 * DA_QK_DIM
DA_V_W = DA_HEADS * DA_V_DIM
RET_QK_W = RET_HEADS * RET_QK_DIM
RET_V_W = RET_HEADS * RET_V_DIM
IN_SPLITS = (DA_QK_W, DA_QK_W, DA_V_W,
             RET_QK_W, RET_QK_W, RET_V_W, RET_V_W,
             SC_WIDTH, SC_WIDTH, SC_WIDTH,
             N_BRANCH * D_MODEL)
IN_WIDTH = 2 * DA_QK_W + DA_V_W + 2 * RET_QK_W + 2 * RET_V_W + 3 * SC_WIDTH + N_BRANCH * D_MODEL

kernel_name = "hybrid_gated_diffattn_retention_shortconv"


def rmsnorm(x, g):
    xf = x.astype(jnp.float32)
    y = xf * lax.rsqrt(jnp.mean(xf * xf, axis=-1, keepdims=True) + EPS)
    return (y * g.astype(jnp.float32)).astype(x.dtype)


def causal_dwconv(x, w, b):
    S = x.shape[1]
    K = w.shape[0]
    xp = jnp.pad(x, ((0, 0), (K - 1, 0), (0, 0)))
    y = b + xp[:, 0:S] * w[0]
    for k in range(1, K):
        y = y + xp[:, k:k + S] * w[k]
    return y


def rel_bucket(n):
    max_exact = REL_BUCKETS // 2
    nf = jnp.maximum(n, 1).astype(jnp.float32)
    large = max_exact + (jnp.log(nf / max_exact) / math.log(REL_MAX_DIST / max_exact)
                         * (REL_BUCKETS - max_exact)).astype(jnp.int32)
    large = jnp.minimum(large, REL_BUCKETS - 1)
    return jnp.where(n < max_exact, n, large)


def rotary(x, pos):
    d = x.shape[-1]
    half = d // 2
    inv = ROPE_THETA ** (-jnp.arange(half, dtype=jnp.float32) / half)
    ang = pos.astype(jnp.float32)[:, None] * inv[None, :]
    cos = jnp.cos(ang)[None, :, None, :]
    sin = jnp.sin(ang)[None, :, None, :]
    xf = x.astype(jnp.float32)
    x1, x2 = xf[..., :half], xf[..., half:]
    return jnp.concatenate([x1 * cos - x2 * sin, x2 * cos + x1 * sin], axis=-1)


def diff_attention(q, k, v, lam, bias_dist):
    B, S, H = q.shape[0], q.shape[1], q.shape[2]
    qh = jnp.transpose(q, (0, 2, 3, 1, 4)).astype(jnp.float32)
    kh = jnp.transpose(k, (0, 2, 3, 1, 4)).astype(jnp.float32)
    vh = jnp.transpose(v, (0, 2, 1, 3)).astype(jnp.float32)
    kpos = jnp.arange(S)
    neg = jnp.finfo(jnp.float32).min

    def block(i):
        start = i * Q_BLOCK
        qb = lax.dynamic_slice_in_dim(qh, start, Q_BLOCK, axis=3)
        dist = (start + jnp.arange(Q_BLOCK))[:, None] - kpos[None, :]
        bias = jnp.transpose(bias_dist[jnp.maximum(dist, 0)], (2, 0, 1)).astype(jnp.float32)
        s = jnp.einsum('bhmqd,bhmkd->bhmqk', qb, kh) + bias[None, :, None]
        s = jnp.where(dist >= 0, s, neg)
        p = jax.nn.softmax(s, axis=-1)
        a = p[:, :, 0] - lam * p[:, :, 1]
        return jnp.einsum('bhqk,bhke->bhqe', a, vh)

    out = lax.map(block, jnp.arange(S // Q_BLOCK))
    return jnp.transpose(out, (1, 0, 3, 2, 4)).reshape(B, S, H, -1)


def retention(q, k, v):
    B, S, H, dk = q.shape
    dv = v.shape[-1]
    C = RET_CHUNK
    N = S // C
    lg = jnp.log(1.0 - 2.0 ** (-5.0 - jnp.arange(H, dtype=jnp.float32)))
    to_chunks = lambda t: jnp.transpose(t.astype(jnp.float32).reshape(B, N, C, H, t.shape[-1]), (0, 3, 1, 2, 4))
    qc, kc, vc = to_chunks(q), to_chunks(k), to_chunks(v)
    idx = jnp.arange(C, dtype=jnp.float32)
    diff = idx[:, None] - idx[None, :]
    d_intra = jnp.where(diff[None] >= 0, jnp.exp(lg[:, None, None] * jnp.maximum(diff, 0.0)[None]), 0.0)
    inner = jnp.einsum('bhncd,bhnjd->bhncj', qc, kc) * d_intra[None, :, None]
    inner = jnp.einsum('bhncj,bhnje->bhnce', inner, vc)
    k_dec = kc * jnp.exp(lg[:, None] * (C - 1 - idx)[None])[None, :, None, :, None]
    kv = jnp.einsum('bhncd,bhnce->nbhde', k_dec, vc)
    chunk_decay = jnp.exp(lg * C)[None, :, None, None]

    def step(R, kv_n):
        return R * chunk_decay + kv_n, R

    _, R_prev = lax.scan(step, jnp.zeros((B, H, dk, dv), jnp.float32), kv)
    q_dec = qc * jnp.exp(lg[:, None] * (idx + 1.0)[None])[None, :, None, :, None]
    cross = jnp.einsum('bhncd,nbhde->bhnce', q_dec, R_prev)
    o = inner + cross
    return jnp.transpose(o, (0, 2, 3, 1, 4)).reshape(B, S, H, dv)


def setup_inputs(seed: int = 0) -> dict:
    key = jax.random.key(seed)
    ks = jax.random.split(key, 24)
    f32 = jnp.float32
    nrm = lambda k, shape, s: jax.random.normal(k, shape, f32) * s
    gain = lambda k, shape: 1.0 + 0.05 * jax.random.normal(k, shape, f32)
    return {
        "x": nrm(ks[0], (BATCH, SEQ, D_MODEL), 1.0),
        "rel_bias": nrm(ks[1], (REL_BUCKETS, DA_HEADS), 0.5),
        "norm_mix_g": gain(ks[2], (DEPTH, D_MODEL)),
        "w_in": nrm(ks[3], (DEPTH, D_MODEL, IN_WIDTH), D_MODEL ** -0.5),
        "b_gate": nrm(ks[4], (DEPTH, N_BRANCH * D_MODEL), 0.02),
        "da_q_norm_g": gain(ks[5], (DEPTH, DA_QK_DIM)),
        "da_k_norm_g": gain(ks[6], (DEPTH, DA_QK_DIM)),
        "da_lambda": nrm(ks[7], (DEPTH, 4, DA_QK_DIM), 0.1),
        "da_subln_g": gain(ks[8], (DEPTH, DA_V_DIM)),
        "ret_norm_g": gain(ks[9], (DEPTH, RET_V_DIM)),
        "sc_conv_w": nrm(ks[10], (DEPTH, CONV_W, SC_WIDTH), CONV_W ** -0.5),
        "sc_conv_b": nrm(ks[11], (DEPTH, SC_WIDTH), 0.02),
        "w_branch_da": nrm(ks[12], (DEPTH, DA_V_W, D_MODEL), DA_V_W ** -0.5),
        "w_branch_ret": nrm(ks[13], (DEPTH, RET_V_W, D_MODEL), RET_V_W ** -0.5),
        "w_branch_sc": nrm(ks[14], (DEPTH, SC_WIDTH, D_MODEL), SC_WIDTH ** -0.5),
        "w_out": nrm(ks[15], (DEPTH, D_MODEL, D_MODEL), 0.5 * D_MODEL ** -0.5),
        "norm_ffn_g": gain(ks[16], (DEPTH, D_MODEL)),
        "w_ffn_in": nrm(ks[17], (DEPTH, D_MODEL, 2 * D_FF), D_MODEL ** -0.5),
        "ffn_conv_w": nrm(ks[18], (DEPTH, CONV_W, D_FF), CONV_W ** -0.5),
        "ffn_conv_b": nrm(ks[19], (DEPTH, D_FF), 0.02),
        "w_ffn_out": nrm(ks[20], (DEPTH, D_FF, D_MODEL), 0.5 * D_FF ** -0.5),
    }


def reference(x, rel_bias, norm_mix_g, w_in, b_gate, da_q_norm_g, da_k_norm_g, da_lambda,
              da_subln_g, ret_norm_g, sc_conv_w, sc_conv_b, w_branch_da, w_branch_ret,
              w_branch_sc, w_out, norm_ffn_g, w_ffn_in, ffn_conv_w, ffn_conv_b, w_ffn_out):
    B, S, D = x.shape
    pos = jnp.arange(S)
    bias_dist = rel_bias[rel_bucket(pos)]
    offs = np.cumsum(IN_SPLITS)[:-1].tolist()
    for l in range(DEPTH):
        h = rmsnorm(x, norm_mix_g[l])
        proj = jnp.einsum('bsd,de->bse', h, w_in[l])
        (da_q, da_k, da_v, r_q, r_k, r_v, r_g,
         sc_b, sc_c, sc_x, gate_pre) = jnp.split(proj, offs, axis=-1)

        lam_init = 0.8 - 0.6 * math.exp(-0.3 * l)
        lp = da_lambda[l].astype(jnp.float32)
        lam = jnp.exp(jnp.sum(lp[0] * lp[1])) - jnp.exp(jnp.sum(lp[2] * lp[3])) + lam_init
        q = rmsnorm(da_q.reshape(B, S, DA_HEADS, 2, DA_QK_DIM), da_q_norm_g[l]) * (DA_QK_DIM ** -0.5)
        k = rmsnorm(da_k.reshape(B, S, DA_HEADS, 2, DA_QK_DIM), da_k_norm_g[l])
        v = da_v.reshape(B, S, DA_HEADS, DA_V_DIM)
        o_da = diff_attention(q, k, v, lam, bias_dist)
        o_da = (rmsnorm(o_da, da_subln_g[l]) * (1.0 - lam_init)).astype(x.dtype).reshape(B, S, DA_V_W)

        rq = rotary(r_q.reshape(B, S, RET_HEADS, RET_QK_DIM), pos)
        rk = rotary(r_k.reshape(B, S, RET_HEADS, RET_QK_DIM), pos) * (RET_QK_DIM ** -0.5)
        rv = r_v.reshape(B, S, RET_HEADS, RET_V_DIM)
        o_ret = rmsnorm(retention(rq, rk, rv), ret_norm_g[l]).reshape(B, S, RET_V_W)
        o_ret = (o_ret * jax.nn.silu(r_g.astype(jnp.float32))).astype(x.dtype)

        o_sc = sc_b * causal_dwconv(sc_c * sc_x, sc_conv_w[l], sc_conv_b[l])

        gates = jax.nn.sigmoid((gate_pre + b_gate[l]).astype(jnp.float32)).astype(x.dtype)
        gates = gates.reshape(B, S, N_BRANCH, D)
        y = (gates[:, :, 0] * jnp.einsum('bse,ed->bsd', o_da, w_branch_da[l])
             + gates[:, :, 1] * jnp.einsum('bse,ed->bsd', o_ret, w_branch_ret[l])
             + gates[:, :, 2] * jnp.einsum('bse,ed->bsd', o_sc, w_branch_sc[l]))
        x = x + jnp.einsum('bsd,de->bse', y, w_out[l])

        h = rmsnorm(x, norm_ffn_g[l])
        gu = jnp.einsum('bsd,df->bsf', h, w_ffn_in[l])
        g_ff, u_ff = gu[..., :D_FF], gu[..., D_FF:]
        g_ff = causal_dwconv(g_ff, ffn_conv_w[l], ffn_conv_b[l])
        x = x + jnp.einsum('bsf,fd->bsd', jax.nn.silu(g_ff) * u_ff, w_ffn_out[l])
    return x
```

```python
import functools
import math

import jax
import jax.numpy as jnp
import numpy as np
from jax import lax
from jax.experimental import pallas as pl
from jax.experimental.pallas import tpu as pltpu

D_MODEL = 1024
DA_HEADS = 4
DA_QK_DIM = 64
DA_V_DIM = 2 * DA_QK_DIM
RET_HEADS = 4
RET_QK_DIM = 64
RET_V_DIM = 64
SC_WIDTH = 256
CONV_W = 3
D_FF = 2816
REL_BUCKETS = 32
REL_MAX_DIST = 128
ROPE_THETA = 10000.0
EPS = 1e-6
N_BRANCH = 3

DA_QK_W = DA_HEADS * 2 * DA_QK_DIM
DA_V_W = DA_HEADS * DA_V_DIM
RET_QK_W = RET_HEADS * RET_QK_DIM
RET_V_W = RET_HEADS * RET_V_DIM

OFF_DA_Q = 0
OFF_DA_K = OFF_DA_Q + DA_QK_W
OFF_DA_V = OFF_DA_K + DA_QK_W
OFF_R_Q = OFF_DA_V + DA_V_W
OFF_R_K = OFF_R_Q + RET_QK_W
OFF_R_V = OFF_R_K + RET_QK_W
OFF_R_G = OFF_R_V + RET_V_W
OFF_SC_B = OFF_R_G + RET_V_W
OFF_SC_C = OFF_SC_B + SC_WIDTH
OFF_SC_X = OFF_SC_C + SC_WIDTH
OFF_GATE = OFF_SC_X + SC_WIDTH
IN_WIDTH = OFF_GATE + N_BRANCH * D_MODEL

LANES = 128
SUBLANES = 8
ROW_BLOCK = 512
ATT_BLOCK = 256
RET_CHUNK = 128
RET_STEP = 512
GATE_CHUNK = 512
FF_CHUNK = 256
N_FF_CHUNKS = D_FF // FF_CHUNK
VMEM_LIMIT = 56 * 1024 * 1024
NEG = -1e30
BF16 = jnp.bfloat16
F32 = jnp.float32


def _resident(shape):
    nd = len(shape)
    return pl.BlockSpec(shape, lambda *_: (0,) * nd, pipeline_mode=pl.Buffered(1))


def _group_mean_sq(y, bd):
    return jnp.dot((y * y).astype(BF16), bd, preferred_element_type=F32) * (1.0 / 64.0)


def _sigmoid(z):
    return 1.0 / (1.0 + jnp.exp(-z))


def _in_proj_kernel(x_ref, g_ref, w_ref, bg_ref, qg_ref, kg_ref, cos_ref, sina_ref, sinb_ref,
                    cw_ref, cb_ref, bd_ref,
                    q_out, k_out, v_out, rq_out, rk_out, rv_out, rg_out, sc_out, gate_out,
                    h_ref, ext_ref, *, blocks_per_seq):
    tm = x_ref.shape[0]
    x = x_ref[...]
    ms = jnp.mean(x * x, axis=-1, keepdims=True)
    h_ref[...] = (x * lax.rsqrt(ms + EPS) * g_ref[...]).astype(BF16)
    bd = bd_ref[...]

    def proj(off, width):
        return jnp.dot(h_ref[...], w_ref[:, off:off + width], preferred_element_type=F32)

    for half in range(DA_QK_W // 256):
        sl = slice(half * 256, (half + 1) * 256)
        y = proj(OFF_DA_Q + half * 256, 256)
        q_out[:, sl] = (y * lax.rsqrt(_group_mean_sq(y, bd) + EPS)
                        * (qg_ref[...] * (DA_QK_DIM ** -0.5))).astype(BF16)
        y = proj(OFF_DA_K + half * 256, 256)
        k_out[:, sl] = (y * lax.rsqrt(_group_mean_sq(y, bd) + EPS) * kg_ref[...]).astype(BF16)
    v_out[...] = proj(OFF_DA_V, DA_V_W).astype(BF16)

    def rotary(y):
        outs = []
        for c in range(RET_QK_W // LANES):
            yc = y[:, c * LANES:(c + 1) * LANES]
            outs.append(yc * cos_ref[...]
                        + pltpu.roll(yc, LANES - 32, axis=1) * sina_ref[...]
                        + pltpu.roll(yc, 32, axis=1) * sinb_ref[...])
        return jnp.concatenate(outs, axis=1)

    rq_out[...] = rotary(proj(OFF_R_Q, RET_QK_W)).astype(BF16)
    rk_out[...] = (rotary(proj(OFF_R_K, RET_QK_W)) * (RET_QK_DIM ** -0.5)).astype(BF16)
    rv_out[...] = proj(OFF_R_V, RET_V_W).astype(BF16)
    rg_out[...] = proj(OFF_R_G, RET_V_W).astype(BF16)

    u = proj(OFF_SC_C, SC_WIDTH) * proj(OFF_SC_X, SC_WIDTH)

    @pl.when(pl.program_id(0) % blocks_per_seq == 0)
    def _():
        ext_ref[0:SUBLANES, :] = jnp.zeros((SUBLANES, SC_WIDTH), F32)

    ext_ref[SUBLANES:SUBLANES + tm, :] = u
    conv = (cb_ref[...]
            + cw_ref[0:1, :] * ext_ref[SUBLANES - 2:SUBLANES - 2 + tm, :]
            + cw_ref[1:2, :] * ext_ref[SUBLANES - 1:SUBLANES - 1 + tm, :]
            + cw_ref[2:3, :] * u)
    ext_ref[0:SUBLANES, :] = u[tm - SUBLANES:, :]
    sc_out[...] = (proj(OFF_SC_B, SC_WIDTH) * conv).astype(BF16)

    for c in range(N_BRANCH * D_MODEL // GATE_CHUNK):
        sl = slice(c * GATE_CHUNK, (c + 1) * GATE_CHUNK)
        z = proj(OFF_GATE + c * GATE_CHUNK, GATE_CHUNK) + bg_ref[:, sl]
        gate_out[:, sl] = _sigmoid(z).astype(BF16)


def _in_proj(x2d, seq, g, w, bg, qg, kg, cos_t, sina_t, sinb_t, cw, cb, bd):
    t = x2d.shape[0]
    tm = ROW_BLOCK
    blocks_per_seq = seq // tm
    row = lambda width: pl.BlockSpec((tm, width), lambda i: (i, 0))
    tab = pl.BlockSpec((tm, LANES), lambda i: (i % blocks_per_seq, 0))
    out_widths = (DA_QK_W, DA_QK_W, DA_V_W, RET_QK_W, RET_QK_W, RET_V_W, RET_V_W, SC_WIDTH,
                  N_BRANCH * D_MODEL)
    return pl.pallas_call(
        functools.partial(_in_proj_kernel, blocks_per_seq=blocks_per_seq),
        grid=(t // tm,),
        in_specs=[row(D_MODEL), _resident(g.shape), _resident(w.shape), _resident(bg.shape),
                  _resident(qg.shape), _resident(kg.shape), tab, tab, tab,
                  _resident(cw.shape), _resident(cb.shape), _resident(bd.shape)],
        out_specs=[row(wd) for wd in out_widths],
        out_shape=[jax.ShapeDtypeStruct((t, wd), BF16) for wd in out_widths],
        scratch_shapes=[pltpu.VMEM((tm, D_MODEL), BF16),
                        pltpu.VMEM((tm + SUBLANES, SC_WIDTH), F32)],
        compiler_params=pltpu.CompilerParams(dimension_semantics=("arbitrary",),
                                             vmem_limit_bytes=VMEM_LIMIT),
        name="in_proj",
    )(x2d, g, w, bg, qg, kg, cos_t, sina_t, sinb_t, cw, cb, bd)


def _diff_attn_kernel(q_ref, k_ref, v_ref, nb_ref, lam_ref, sg_ref, o_ref, *, lam_init):
    tq = q_ref.shape[1]
    qi = pl.program_id(2)
    q = q_ref[0]
    lane = lax.broadcasted_iota(jnp.int32, q.shape, 1)
    zero = jnp.zeros_like(q)
    qs = jnp.concatenate([jnp.where(lane < DA_QK_DIM, q, zero),
                          jnp.where(lane >= DA_QK_DIM, q, zero)], axis=0)

    def step(j, carry, bias):
        m, l, acc = carry
        start = pl.multiple_of(j * tq, tq)
        k = k_ref[0, pl.ds(start, tq), :]
        v = v_ref[0, pl.ds(start, tq), :]
        s = lax.dot_general(qs, k, (((1,), (1,)), ((), ())), preferred_element_type=F32)
        if bias is not None:
            s = s + jnp.concatenate([bias, bias], axis=0)
        m_new = jnp.maximum(m, jnp.max(s, axis=1, keepdims=True))
        alpha = jnp.exp(m - m_new)
        p = jnp.exp(s - m_new)
        l = alpha * l + jnp.sum(p, axis=1, keepdims=True)
        acc = alpha * acc + jnp.dot(p.astype(BF16), v, preferred_element_type=F32)
        return m_new, l, acc

    carry = (jnp.full((2 * tq, 1), NEG, F32), jnp.zeros((2 * tq, 1), F32),
             jnp.zeros((2 * tq, DA_V_DIM), F32))
    carry = lax.fori_loop(0, jnp.maximum(qi - 1, 0), lambda j, c: step(j, c, None), carry)
    carry = lax.cond(qi > 0, lambda c: step(qi - 1, c, nb_ref[0, 1]), lambda c: c, carry)
    _, l, acc = step(qi, carry, nb_ref[0, 0])

    lp = lam_ref[...]
    lam = (jnp.exp(jnp.sum(lp[0:1] * lp[1:2], axis=1, keepdims=True))
           - jnp.exp(jnp.sum(lp[2:3] * lp[3:4], axis=1, keepdims=True)) + lam_init)
    o = acc[:tq] / l[:tq] - lam * (acc[tq:] / l[tq:])
    ms = jnp.mean(o * o, axis=-1, keepdims=True)
    o_ref[0] = (o * lax.rsqrt(ms + EPS) * sg_ref[...] * (1.0 - lam_init)).astype(BF16)


def _diff_attn(q, k, v, nb, lam_p, sg, lam_init):
    b, s, _ = q.shape
    tq = ATT_BLOCK
    return pl.pallas_call(
        functools.partial(_diff_attn_kernel, lam_init=lam_init),
        grid=(b, DA_HEADS, s // tq),
        in_specs=[pl.BlockSpec((1, tq, LANES), lambda bi, h, i: (bi, i, h)),
                  pl.BlockSpec((1, s, LANES), lambda bi, h, i: (bi, 0, h)),
                  pl.BlockSpec((1, s, LANES), lambda bi, h, i: (bi, 0, h)),
                  pl.BlockSpec((1, 2, tq, tq), lambda bi, h, i: (h, 0, 0, 0)),
                  pl.BlockSpec(lam_p.shape, lambda bi, h, i: (0, 0)),
                  pl.BlockSpec(sg.shape, lambda bi, h, i: (0, 0))],
        out_specs=pl.BlockSpec((1, tq, LANES), lambda bi, h, i: (bi, i, h)),
        out_shape=jax.ShapeDtypeStruct((b, s, DA_V_W), BF16),
        compiler_params=pltpu.CompilerParams(
            dimension_semantics=("arbitrary", "arbitrary", "arbitrary"),
            vmem_limit_bytes=VMEM_LIMIT),
        name="diff_attn",
    )(q, k, v, nb, lam_p, sg)


def _ret_log_gamma(head):
    lg = [math.log(1.0 - 2.0 ** (-5.0 - h)) for h in range(RET_HEADS)]
    out = jnp.full(head.shape, lg[RET_HEADS - 1], F32)
    for h in range(RET_HEADS - 2, -1, -1):
        out = jnp.where(head == h, lg[h], out)
    return out


def _retention_kernel(q_ref, k_ref, v_ref, gate_ref, ng_ref, bd_ref, o_ref, state_ref):
    c = RET_CHUNK
    w = RET_QK_W
    n_chunks = q_ref.shape[1] // c

    @pl.when(pl.program_id(1) == 0)
    def _():
        state_ref[...] = jnp.zeros_like(state_ref)

    head_shift = RET_QK_DIM.bit_length() - 1
    chunk_shift = c.bit_length() - 1
    assert 1 << head_shift == RET_QK_DIM == RET_V_DIM and 1 << chunk_shift == c
    lane_head = lax.broadcasted_iota(jnp.int32, (c, w), 1) >> head_shift
    row = lax.broadcasted_iota(jnp.int32, (c, w), 0).astype(F32)
    lg_lane = _ret_log_gamma(lane_head)
    q_decay = jnp.exp(lg_lane * (row + 1.0))
    k_decay = jnp.exp(lg_lane * (c - 1.0 - row))
    chunk_decay = jnp.exp(lg_lane[0:1, :] * float(c))
    srow = lax.broadcasted_iota(jnp.int32, (RET_HEADS * c, c), 0)
    scol = lax.broadcasted_iota(jnp.int32, (RET_HEADS * c, c), 1)
    diff = ((srow & (c - 1)) - scol).astype(F32)
    d_intra = jnp.where(diff >= 0,
                        jnp.exp(_ret_log_gamma(srow >> chunk_shift) * jnp.maximum(diff, 0.0)), 0.0)
    st_row_head = lax.broadcasted_iota(jnp.int32, (w, w), 0) >> head_shift
    st_col_head = lax.broadcasted_iota(jnp.int32, (w, w), 1) >> head_shift
    same_head = st_row_head == st_col_head

    for n in range(n_chunks):
        sl = slice(n * c, (n + 1) * c)
        q = q_ref[0, sl, :]
        k = k_ref[0, sl, :]
        v = v_ref[0, sl, :]
        zero = jnp.zeros_like(q)
        qs = jnp.concatenate([jnp.where(lane_head == h, q, zero) for h in range(RET_HEADS)], axis=0)
        s = lax.dot_general(qs, k, (((1,), (1,)), ((), ())), preferred_element_type=F32)
        o_all = jnp.dot((s * d_intra).astype(BF16), v, preferred_element_type=F32)
        inner = jnp.zeros((c, w), F32)
        for h in range(RET_HEADS):
            inner = jnp.where(lane_head == h, o_all[h * c:(h + 1) * c], inner)
        state = state_ref[...]
        cross = jnp.dot(q, state.astype(BF16), preferred_element_type=F32) * q_decay
        o = inner + cross
        k_dec = (k.astype(F32) * k_decay).T.astype(BF16)
        kv = jnp.dot(k_dec, v, preferred_element_type=F32)
        state_ref[...] = state * chunk_decay + jnp.where(same_head, kv, 0.0)

        y = o * lax.rsqrt(_group_mean_sq(o, bd_ref[...]) + EPS) * ng_ref[...]
        gate = gate_ref[0, sl, :].astype(F32)
        o_ref[0, sl, :] = (y * (gate * _sigmoid(gate))).astype(BF16)


def _retention(rq, rk, rv, rg, ng, bd):
    b, s, w = rq.shape
    ts = RET_STEP
    blk = pl.BlockSpec((1, ts, w), lambda bi, i: (bi, i, 0))
    return pl.pallas_call(
        _retention_kernel,
        grid=(b, s // ts),
        in_specs=[blk, blk, blk, blk, _resident(ng.shape), _resident(bd.shape)],
        out_specs=blk,
        out_shape=jax.ShapeDtypeStruct((b, s, w), BF16),
        scratch_shapes=[pltpu.VMEM((w, w), F32)],
        compiler_params=pltpu.CompilerParams(dimension_semantics=("arbitrary", "arbitrary"),
                                             vmem_limit_bytes=VMEM_LIMIT),
        name="retention",
    )(rq, rk, rv, rg, ng, bd)


def _merge_ffn_kernel(x_ref, oda_ref, oret_ref, osc_ref, gate_ref, wda_ref, wret_ref, wsc_ref,
                      wout_ref, ng_ref, wg_ref, wu_ref, cw_ref, cb_ref, wo_ref,
                      out_ref, h_ref, ext_ref, carry_ref, *, blocks_per_seq):
    tm = x_ref.shape[0]
    d = D_MODEL
    y = (gate_ref[:, 0:d].astype(F32)
         * jnp.dot(oda_ref[...], wda_ref[...], preferred_element_type=F32))
    y += (gate_ref[:, d:2 * d].astype(F32)
          * jnp.dot(oret_ref[...], wret_ref[...], preferred_element_type=F32))
    y += (gate_ref[:, 2 * d:3 * d].astype(F32)
          * jnp.dot(osc_ref[...], wsc_ref[...], preferred_element_type=F32))
    x1 = x_ref[...] + jnp.dot(y.astype(BF16), wout_ref[...], preferred_element_type=F32)
    out_ref[...] = x1
    ms = jnp.mean(x1 * x1, axis=-1, keepdims=True)
    h_ref[...] = (x1 * lax.rsqrt(ms + EPS) * ng_ref[...]).astype(BF16)

    @pl.when(pl.program_id(0) % blocks_per_seq == 0)
    def _():
        carry_ref[...] = jnp.zeros_like(carry_ref)

    def chunk(c, _):
        g = jnp.dot(h_ref[...], wg_ref[c], preferred_element_type=F32)
        u = jnp.dot(h_ref[...], wu_ref[c], preferred_element_type=F32)
        ext_ref[0:SUBLANES, :] = carry_ref[c]
        ext_ref[SUBLANES:SUBLANES + tm, :] = g
        cw = cw_ref[c]
        gc = (cb_ref[c]
              + cw[0:1, :] * ext_ref[SUBLANES - 2:SUBLANES - 2 + tm, :]
              + cw[1:2, :] * ext_ref[SUBLANES - 1:SUBLANES - 1 + tm, :]
              + cw[2:3, :] * g)
        carry_ref[c] = g[tm - SUBLANES:, :]
        a = (gc * _sigmoid(gc) * u).astype(BF16)
        out_ref[...] += jnp.dot(a, wo_ref[c], preferred_element_type=F32)
        return 0

    lax.fori_loop(0, N_FF_CHUNKS, chunk, 0)


def _merge_ffn(x2d, seq, oda, oret, osc, gates, wda, wret, wsc, wout, ng, wg, wu, cw, cb, wo):
    t = x2d.shape[0]
    tm = ROW_BLOCK
    blocks_per_seq = seq // tm
    row = lambda width: pl.BlockSpec((tm, width), lambda i: (i, 0))
    return pl.pallas_call(
        functools.partial(_merge_ffn_kernel, blocks_per_seq=blocks_per_seq),
        grid=(t // tm,),
        in_specs=[row(D_MODEL), row(DA_V_W), row(RET_V_W), row(SC_WIDTH), row(N_BRANCH * D_MODEL)]
                 + [_resident(a.shape) for a in (wda, wret, wsc, wout, ng, wg, wu, cw, cb, wo)],
        out_specs=row(D_MODEL),
        out_shape=jax.ShapeDtypeStruct((t, D_MODEL), F32),
        scratch_shapes=[pltpu.VMEM((tm, D_MODEL), BF16),
                        pltpu.VMEM((tm + SUBLANES, FF_CHUNK), F32),
                        pltpu.VMEM((N_FF_CHUNKS, SUBLANES, FF_CHUNK), F32)],
        compiler_params=pltpu.CompilerParams(dimension_semantics=("arbitrary",),
                                             vmem_limit_bytes=VMEM_LIMIT),
        name="merge_ffn",
    )(x2d, oda, oret, osc, gates, wda, wret, wsc, wout, ng, wg, wu, cw, cb, wo)


def _rel_bucket(n):
    max_exact = REL_BUCKETS // 2
    nf = jnp.maximum(n, 1).astype(F32)
    large = max_exact + (jnp.log(nf / max_exact) / math.log(REL_MAX_DIST / max_exact)
                         * (REL_BUCKETS - max_exact)).astype(jnp.int32)
    large = jnp.minimum(large, REL_BUCKETS - 1)
    return jnp.where(n < max_exact, n, large)


def _near_bias_tiles(rel_bias, tq):
    assert tq >= REL_MAX_DIST
    dist = jnp.arange(2 * tq)
    shifted = rel_bias[_rel_bucket(dist)] - rel_bias[REL_BUCKETS - 1]
    r = jnp.arange(tq)[:, None]
    c = jnp.arange(tq)[None, :]
    diag = jnp.where((r >= c)[None], shifted[jnp.maximum(r - c, 0)].transpose(2, 0, 1), NEG)
    prev = shifted[tq + r - c].transpose(2, 0, 1)
    return jnp.stack([diag, prev], axis=1).astype(F32)


def _rotary_tables(seq):
    half = RET_QK_DIM // 2
    inv = ROPE_THETA ** (-jnp.arange(half, dtype=F32) / half)
    ang = jnp.arange(seq).astype(F32)[:, None] * inv[None, :]
    cos, sin = jnp.cos(ang), jnp.sin(ang)
    zero = jnp.zeros_like(sin)
    reps = LANES // RET_QK_DIM
    cos_t = jnp.tile(jnp.concatenate([cos, cos], axis=1), (1, reps))
    sina_t = jnp.tile(jnp.concatenate([-sin, zero], axis=1), (1, reps))
    sinb_t = jnp.tile(jnp.concatenate([zero, sin], axis=1), (1, reps))
    return cos_t, sina_t, sinb_t


def _ff_chunks_cols(w):
    return w.reshape(w.shape[0], N_FF_CHUNKS, FF_CHUNK).transpose(1, 0, 2)


@jax.jit
def _forward(x, rel_bias, norm_mix_g, w_in, b_gate, da_q_norm_g, da_k_norm_g, da_lambda,
             da_subln_g, ret_norm_g, sc_conv_w, sc_conv_b, w_branch_da, w_branch_ret,
             w_branch_sc, w_out, norm_ffn_g, w_ffn_in, ffn_conv_w, ffn_conv_b, w_ffn_out):
    b, s, d = x.shape
    depth = w_in.shape[0]
    assert d == D_MODEL and s % ROW_BLOCK == 0 and s % ATT_BLOCK == 0 and s % RET_STEP == 0
    nb = _near_bias_tiles(rel_bias.astype(F32), ATT_BLOCK)
    cos_t, sina_t, sinb_t = _rotary_tables(s)
    group = jnp.arange(256) // 64
    bd = (group[:, None] == group[None, :]).astype(BF16)

    x2d = x.reshape(b * s, d)
    for l in range(depth):
        lam_init = 0.8 - 0.6 * math.exp(-0.3 * l)
        q, k, v, rq, rk, rv, rg, osc, gates = _in_proj(
            x2d, s, norm_mix_g[l][None, :], w_in[l].astype(BF16), b_gate[l][None, :],
            jnp.tile(da_q_norm_g[l], 256 // DA_QK_DIM)[None, :],
            jnp.tile(da_k_norm_g[l], 256 // DA_QK_DIM)[None, :],
            cos_t, sina_t, sinb_t, sc_conv_w[l], sc_conv_b[l][None, :], bd)
        to3 = lambda a: a.reshape(b, s, a.shape[-1])
        oda = _diff_attn(to3(q), to3(k), to3(v), nb, da_lambda[l].astype(F32),
                         da_subln_g[l][None, :], lam_init)
        oret = _retention(to3(rq), to3(rk), to3(rv), to3(rg),
                          jnp.tile(ret_norm_g[l], RET_HEADS)[None, :], bd)
        x2d = _merge_ffn(
            x2d, s, oda.reshape(b * s, DA_V_W), oret.reshape(b * s, RET_V_W), osc, gates,
            w_branch_da[l].astype(BF16), w_branch_ret[l].astype(BF16), w_branch_sc[l].astype(BF16),
            w_out[l].astype(BF16), norm_ffn_g[l][None, :],
            _ff_chunks_cols(w_ffn_in[l][:, :D_FF].astype(BF16)),
            _ff_chunks_cols(w_ffn_in[l][:, D_FF:].astype(BF16)),
            ffn_conv_w[l].reshape(CONV_W, N_FF_CHUNKS, FF_CHUNK).transpose(1, 0, 2),
            ffn_conv_b[l].reshape(N_FF_CHUNKS, 1, FF_CHUNK),
            w_ffn_out[l].astype(BF16).reshape(N_FF_CHUNKS, FF_CHUNK, D_MODEL))
    return x2d.reshape(b, s, d)


def kernel(x, rel_bias, norm_mix_g, w_in, b_gate, da_q_norm_g, da_k_norm_g, da_lambda, da_subln_g,
           ret_norm_g, sc_conv_w, sc_conv_b, w_branch_da, w_branch_ret, w_branch_sc, w_out,
           norm_ffn_g, w_ffn_in, ffn_conv_w, ffn_conv_b, w_ffn_out):
    return _forward(x, rel_bias, norm_mix_g, w_in, b_gate, da_q_norm_g, da_k_norm_g, da_lambda,
                    da_subln_g, ret_norm_g, sc_conv_w, sc_conv_b, w_branch_da, w_branch_ret,
                    w_branch_sc, w_out, norm_ffn_g, w_ffn_in, ffn_conv_w, ffn_conv_b, w_ffn_out)
```

```python
import functools
import math

import jax
import jax.numpy as jnp
import numpy as np
from jax import lax
from jax.experimental import pallas as pl
from jax.experimental.pallas import tpu as pltpu

D_MODEL = 1024
DA_HEADS = 4
DA_QK_DIM = 64
DA_V_DIM = 2 * DA_QK_DIM
RET_HEADS = 4
RET_QK_DIM = 64
RET_V_DIM = 64
SC_WIDTH = 256
CONV_W = 3
D_FF = 2816
REL_BUCKETS = 32
REL_MAX_DIST = 128
ROPE_THETA = 10000.0
EPS = 1e-6
N_BRANCH = 3

DA_QK_W = DA_HEADS * 2 * DA_QK_DIM
DA_V_W = DA_HEADS * DA_V_DIM
RET_QK_W = RET_HEADS * RET_QK_DIM
RET_V_W = RET_HEADS * RET_V_DIM

OFF_DA_Q = 0
OFF_DA_K = OFF_DA_Q + DA_QK_W
OFF_DA_V = OFF_DA_K + DA_QK_W
OFF_R_Q = OFF_DA_V + DA_V_W
OFF_R_K = OFF_R_Q + RET_QK_W
OFF_R_V = OFF_R_K + RET_QK_W
OFF_R_G = OFF_R_V + RET_V_W
OFF_SC_B = OFF_R_G + RET_V_W
OFF_SC_C = OFF_SC_B + SC_WIDTH
OFF_SC_X = OFF_SC_C + SC_WIDTH
OFF_GATE = OFF_SC_X + SC_WIDTH
IN_WIDTH = OFF_GATE + N_BRANCH * D_MODEL

LANES = 128
SUBLANES = 8
ROW_BLOCK = 512
ATT_BLOCK = 256
RET_CHUNK = 128
RET_STEP = 512
GATE_CHUNK = 512
FF_CHUNK = 256
N_FF_CHUNKS = D_FF // FF_CHUNK
VMEM_LIMIT = 56 * 1024 * 1024
NEG = -1e30
LOG2E = math.log2(math.e)
BF16 = jnp.bfloat16
F32 = jnp.float32


def _resident(shape):
    nd = len(shape)
    return pl.BlockSpec(shape, lambda *_: (0,) * nd, pipeline_mode=pl.Buffered(1))


def _group_mean_sq(y, bd):
    return jnp.dot((y * y).astype(BF16), bd, preferred_element_type=F32) * (1.0 / 64.0)


def _sigmoid(z):
    return 1.0 / (1.0 + jnp.exp(-z))


def _in_proj_kernel(x_ref, g_ref, w_ref, bg_ref, qg_ref, kg_ref, cos_ref, sina_ref, sinb_ref,
                    cw_ref, cb_ref, bd_ref,
                    q_out, k_out, vt_out, rq_out, rk_out, rv_out, rg_out, sc_out, gate_out,
                    h_ref, ext_ref, *, blocks_per_seq):
    tm = x_ref.shape[0]
    x = x_ref[...]
    ms = jnp.mean(x * x, axis=-1, keepdims=True)
    h_ref[...] = (x * lax.rsqrt(ms + EPS) * g_ref[...]).astype(BF16)
    bd = bd_ref[...]

    def proj(off, width):
        return jnp.dot(h_ref[...], w_ref[:, off:off + width], preferred_element_type=F32)

    for half in range(DA_QK_W // 256):
        sl = slice(half * 256, (half + 1) * 256)
        y = proj(OFF_DA_Q + half * 256, 256)
        q_out[:, sl] = (y * lax.rsqrt(_group_mean_sq(y, bd) + EPS)
                        * (qg_ref[...] * (DA_QK_DIM ** -0.5 * LOG2E))).astype(BF16)
        y = proj(OFF_DA_K + half * 256, 256)
        k_out[:, sl] = (y * lax.rsqrt(_group_mean_sq(y, bd) + EPS) * kg_ref[...]).astype(BF16)
    v = proj(OFF_DA_V, DA_V_W)
    for c in range(tm // ATT_BLOCK):
        vt_out[0, c] = v[c * ATT_BLOCK:(c + 1) * ATT_BLOCK, :].T.astype(BF16)

    def rotary(y):
        outs = []
        for c in range(RET_QK_W // LANES):
            yc = y[:, c * LANES:(c + 1) * LANES]
            outs.append(yc * cos_ref[...]
                        + pltpu.roll(yc, LANES - 32, axis=1) * sina_ref[...]
                        + pltpu.roll(yc, 32, axis=1) * sinb_ref[...])
        return jnp.concatenate(outs, axis=1)

    rq_out[...] = rotary(proj(OFF_R_Q, RET_QK_W)).astype(BF16)
    rk_out[...] = (rotary(proj(OFF_R_K, RET_QK_W)) * (RET_QK_DIM ** -0.5)).astype(BF16)
    rv_out[...] = proj(OFF_R_V, RET_V_W).astype(BF16)
    rg_out[...] = proj(OFF_R_G, RET_V_W).astype(BF16)

    u = proj(OFF_SC_C, SC_WIDTH) * proj(OFF_SC_X, SC_WIDTH)

    @pl.when(pl.program_id(0) % blocks_per_seq == 0)
    def _():
        ext_ref[0:SUBLANES, :] = jnp.zeros((SUBLANES, SC_WIDTH), F32)

    ext_ref[SUBLANES:SUBLANES + tm, :] = u
    conv = (cb_ref[...]
            + cw_ref[0:1, :] * ext_ref[SUBLANES - 2:SUBLANES - 2 + tm, :]
            + cw_ref[1:2, :] * ext_ref[SUBLANES - 1:SUBLANES - 1 + tm, :]
            + cw_ref[2:3, :] * u)
    ext_ref[0:SUBLANES, :] = u[tm - SUBLANES:, :]
    sc_out[...] = (proj(OFF_SC_B, SC_WIDTH) * conv).astype(BF16)

    for c in range(N_BRANCH * D_MODEL // GATE_CHUNK):
        sl = slice(c * GATE_CHUNK, (c + 1) * GATE_CHUNK)
        z = proj(OFF_GATE + c * GATE_CHUNK, GATE_CHUNK) + bg_ref[:, sl]
        gate_out[:, sl] = _sigmoid(z).astype(BF16)


def _in_proj(x2d, seq, g, w, bg, qg, kg, cos_t, sina_t, sinb_t, cw, cb, bd):
    t = x2d.shape[0]
    tm = ROW_BLOCK
    blocks_per_seq = seq // tm
    row = lambda width: pl.BlockSpec((tm, width), lambda i: (i, 0))
    tab = pl.BlockSpec((tm, LANES), lambda i: (i % blocks_per_seq, 0))
    out_widths = (DA_QK_W, DA_QK_W, RET_QK_W, RET_QK_W, RET_V_W, RET_V_W, SC_WIDTH,
                  N_BRANCH * D_MODEL)
    out_specs = [row(wd) for wd in out_widths]
    out_shape = [jax.ShapeDtypeStruct((t, wd), BF16) for wd in out_widths]
    out_specs.insert(2, pl.BlockSpec((1, tm // ATT_BLOCK, DA_V_W, ATT_BLOCK),
                                     lambda i: (i // blocks_per_seq, i % blocks_per_seq, 0, 0)))
    out_shape.insert(2, jax.ShapeDtypeStruct((t // seq, seq // ATT_BLOCK, DA_V_W, ATT_BLOCK), BF16))
    return pl.pallas_call(
        functools.partial(_in_proj_kernel, blocks_per_seq=blocks_per_seq),
        grid=(t // tm,),
        in_specs=[row(D_MODEL), _resident(g.shape), _resident(w.shape), _resident(bg.shape),
                  _resident(qg.shape), _resident(kg.shape), tab, tab, tab,
                  _resident(cw.shape), _resident(cb.shape), _resident(bd.shape)],
        out_specs=out_specs,
        out_shape=out_shape,
        scratch_shapes=[pltpu.VMEM((tm, D_MODEL), BF16),
                        pltpu.VMEM((tm + SUBLANES, SC_WIDTH), F32)],
        compiler_params=pltpu.CompilerParams(dimension_semantics=("arbitrary",),
                                             vmem_limit_bytes=VMEM_LIMIT),
        name="in_proj",
    )(x2d, g, w, bg, qg, kg, cos_t, sina_t, sinb_t, cw, cb, bd)


def _diff_attn_kernel(q_ref, k_ref, vt_ref, nb_ref, lam_ref, sg_ref, o_ref, *, lam_init):
    tq = q_ref.shape[1]
    qi = pl.program_id(2)
    q = q_ref[0]
    lane = lax.broadcasted_iota(jnp.int32, q.shape, 1)
    zero = jnp.zeros_like(q)
    qs = jnp.concatenate([jnp.where(lane < DA_QK_DIM, q, zero),
                          jnp.where(lane >= DA_QK_DIM, q, zero)], axis=0)

    def scores(j):
        start = pl.multiple_of(j * tq, tq)
        return lax.dot_general(k_ref[0, pl.ds(start, tq), :], qs, (((1,), (1,)), ((), ())),
                               preferred_element_type=F32)

    def update(s, j, carry):
        m, l, acc = carry
        m_new = jnp.maximum(m, jnp.max(s, axis=0, keepdims=True))
        alpha = jnp.exp2(m - m_new)
        p = jnp.exp2(s - m_new)
        l = alpha * l + jnp.sum(p, axis=0, keepdims=True)
        acc = alpha * acc + jnp.dot(vt_ref[0, j], p.astype(BF16), preferred_element_type=F32)
        return m_new, l, acc

    def far(j, state):
        s, carry = state
        return scores(j + 1), update(s, j, carry)

    def prev(state):
        s, carry = state
        return scores(qi), update(s + nb_ref[0, 1], qi - 1, carry)

    carry = (jnp.full((1, 2 * tq), NEG, F32), jnp.zeros((1, 2 * tq), F32),
             jnp.zeros((DA_V_DIM, 2 * tq), F32))
    state = lax.fori_loop(0, jnp.maximum(qi - 1, 0), far, (scores(0), carry))
    s, carry = lax.cond(qi > 0, prev, lambda st: st, state)
    _, l, acc = update(s + nb_ref[0, 0], qi, carry)

    lp = lam_ref[...]
    lam = (jnp.exp(jnp.sum(lp[0:1] * lp[1:2], axis=1, keepdims=True))
           - jnp.exp(jnp.sum(lp[2:3] * lp[3:4], axis=1, keepdims=True)) + lam_init)
    inv_l = 1.0 / l
    o = acc[:, :tq] * inv_l[:, :tq] - lam * (acc[:, tq:] * inv_l[:, tq:])
    ms = jnp.mean(o * o, axis=0, keepdims=True)
    o_ref[0] = (o * lax.rsqrt(ms + EPS) * (sg_ref[...] * (1.0 - lam_init))).T.astype(BF16)


def _diff_attn(q, k, vt, nb, lam_p, sg, lam_init):
    b, s, _ = q.shape
    tq = ATT_BLOCK
    return pl.pallas_call(
        functools.partial(_diff_attn_kernel, lam_init=lam_init),
        grid=(b, DA_HEADS, s // tq),
        in_specs=[pl.BlockSpec((1, tq, LANES), lambda bi, h, i: (bi, i, h)),
                  pl.BlockSpec((1, s, LANES), lambda bi, h, i: (bi, 0, h)),
                  pl.BlockSpec((1, s // tq, DA_V_DIM, tq), lambda bi, h, i: (bi, 0, h, 0)),
                  pl.BlockSpec((1, 2, tq, 2 * tq), lambda bi, h, i: (h, 0, 0, 0)),
                  pl.BlockSpec(lam_p.shape, lambda bi, h, i: (0, 0)),
                  pl.BlockSpec(sg.shape, lambda bi, h, i: (0, 0))],
        out_specs=pl.BlockSpec((1, tq, LANES), lambda bi, h, i: (bi, i, h)),
        out_shape=jax.ShapeDtypeStruct((b, s, DA_V_W), BF16),
        compiler_params=pltpu.CompilerParams(
            dimension_semantics=("arbitrary", "arbitrary", "arbitrary"),
            vmem_limit_bytes=VMEM_LIMIT),
        name="diff_attn",
    )(q, k, vt, nb, lam_p, sg)


def _ret_log_gamma(head):
    lg = [math.log(1.0 - 2.0 ** (-5.0 - h)) for h in range(RET_HEADS)]
    out = jnp.full(head.shape, lg[RET_HEADS - 1], F32)
    for h in range(RET_HEADS - 2, -1, -1):
        out = jnp.where(head == h, lg[h], out)
    return out


def _retention_kernel(q_ref, k_ref, v_ref, gate_ref, ng_ref, bd_ref, o_ref, state_ref):
    c = RET_CHUNK
    w = RET_QK_W
    n_chunks = q_ref.shape[1] // c

    @pl.when(pl.program_id(1) == 0)
    def _():
        state_ref[...] = jnp.zeros_like(state_ref)

    head_shift = RET_QK_DIM.bit_length() - 1
    chunk_shift = c.bit_length() - 1
    assert 1 << head_shift == RET_QK_DIM == RET_V_DIM and 1 << chunk_shift == c
    lane_head = lax.broadcasted_iota(jnp.int32, (c, w), 1) >> head_shift
    row = lax.broadcasted_iota(jnp.int32, (c, w), 0).astype(F32)
    lg_lane = _ret_log_gamma(lane_head)
    q_decay = jnp.exp(lg_lane * (row + 1.0))
    k_decay = jnp.exp(lg_lane * (c - 1.0 - row))
    chunk_decay = jnp.exp(lg_lane[0:1, :] * float(c))
    srow = lax.broadcasted_iota(jnp.int32, (RET_HEADS * c, c), 0)
    scol = lax.broadcasted_iota(jnp.int32, (RET_HEADS * c, c), 1)
    diff = ((srow & (c - 1)) - scol).astype(F32)
    d_intra = jnp.where(diff >= 0,
                        jnp.exp(_ret_log_gamma(srow >> chunk_shift) * jnp.maximum(diff, 0.0)), 0.0)
    st_row_head = lax.broadcasted_iota(jnp.int32, (w, w), 0) >> head_shift
    st_col_head = lax.broadcasted_iota(jnp.int32, (w, w), 1) >> head_shift
    same_head = st_row_head == st_col_head

    for n in range(n_chunks):
        sl = slice(n * c, (n + 1) * c)
        q = q_ref[0, sl, :]
        k = k_ref[0, sl, :]
        v = v_ref[0, sl, :]
        zero = jnp.zeros_like(q)
        qs = jnp.concatenate([jnp.where(lane_head == h, q, zero) for h in range(RET_HEADS)], axis=0)
        s = lax.dot_general(qs, k, (((1,), (1,)), ((), ())), preferred_element_type=F32)
        o_all = jnp.dot((s * d_intra).astype(BF16), v, preferred_element_type=F32)
        inner = jnp.zeros((c, w), F32)
        for h in range(RET_HEADS):
            inner = jnp.where(lane_head == h, o_all[h * c:(h + 1) * c], inner)
        state = state_ref[...]
        cross = jnp.dot(q, state.astype(BF16), preferred_element_type=F32) * q_decay
        o = inner + cross
        k_dec = (k.astype(F32) * k_decay).T.astype(BF16)
        kv = jnp.dot(k_dec, v, preferred_element_type=F32)
        state_ref[...] = state * chunk_decay + jnp.where(same_head, kv, 0.0)

        y = o * lax.rsqrt(_group_mean_sq(o, bd_ref[...]) + EPS) * ng_ref[...]
        gate = gate_ref[0, sl, :].astype(F32)
        o_ref[0, sl, :] = (y * (gate * _sigmoid(gate))).astype(BF16)


def _retention(rq, rk, rv, rg, ng, bd):
    b, s, w = rq.shape
    ts = RET_STEP
    blk = pl.BlockSpec((1, ts, w), lambda bi, i: (bi, i, 0))
    return pl.pallas_call(
        _retention_kernel,
        grid=(b, s // ts),
        in_specs=[blk, blk, blk, blk, _resident(ng.shape), _resident(bd.shape)],
        out_specs=blk,
        out_shape=jax.ShapeDtypeStruct((b, s, w), BF16),
        scratch_shapes=[pltpu.VMEM((w, w), F32)],
        compiler_params=pltpu.CompilerParams(dimension_semantics=("arbitrary", "arbitrary"),
                                             vmem_limit_bytes=VMEM_LIMIT),
        name="retention",
    )(rq, rk, rv, rg, ng, bd)


def _merge_ffn_kernel(x_ref, oda_ref, oret_ref, osc_ref, gate_ref, wda_ref, wret_ref, wsc_ref,
                      wout_ref, ng_ref, wg_ref, wu_ref, cw_ref, cb_ref, wo_ref,
                      out_ref, h_ref, ext_ref, carry_ref, *, blocks_per_seq):
    tm = x_ref.shape[0]
    d = D_MODEL
    y = (gate_ref[:, 0:d].astype(F32)
         * jnp.dot(oda_ref[...], wda_ref[...], preferred_element_type=F32))
    y += (gate_ref[:, d:2 * d].astype(F32)
          * jnp.dot(oret_ref[...], wret_ref[...], preferred_element_type=F32))
    y += (gate_ref[:, 2 * d:3 * d].astype(F32)
          * jnp.dot(osc_ref[...], wsc_ref[...], preferred_element_type=F32))
    x1 = x_ref[...] + jnp.dot(y.astype(BF16), wout_ref[...], preferred_element_type=F32)
    out_ref[...] = x1
    ms = jnp.mean(x1 * x1, axis=-1, keepdims=True)
    h_ref[...] = (x1 * lax.rsqrt(ms + EPS) * ng_ref[...]).astype(BF16)

    @pl.when(pl.program_id(0) % blocks_per_seq == 0)
    def _():
        carry_ref[...] = jnp.zeros_like(carry_ref)

    def chunk(c, _):
        g = jnp.dot(h_ref[...], wg_ref[c], preferred_element_type=F32)
        u = jnp.dot(h_ref[...], wu_ref[c], preferred_element_type=F32)
        ext_ref[0:SUBLANES, :] = carry_ref[c]
        ext_ref[SUBLANES:SUBLANES + tm, :] = g
        cw = cw_ref[c]
        gc = (cb_ref[c]
              + cw[0:1, :] * ext_ref[SUBLANES - 2:SUBLANES - 2 + tm, :]
              + cw[1:2, :] * ext_ref[SUBLANES - 1:SUBLANES - 1 + tm, :]
              + cw[2:3, :] * g)
        carry_ref[c] = g[tm - SUBLANES:, :]
        a = (gc * _sigmoid(gc) * u).astype(BF16)
        out_ref[...] += jnp.dot(a, wo_ref[c], preferred_element_type=F32)
        return 0

    lax.fori_loop(0, N_FF_CHUNKS, chunk, 0)


def _merge_ffn(x2d, seq, oda, oret, osc, gates, wda, wret, wsc, wout, ng, wg, wu, cw, cb, wo):
    t = x2d.shape[0]
    tm = ROW_BLOCK
    blocks_per_seq = seq // tm
    row = lambda width: pl.BlockSpec((tm, width), lambda i: (i, 0))
    return pl.pallas_call(
        functools.partial(_merge_ffn_kernel, blocks_per_seq=blocks_per_seq),
        grid=(t // tm,),
        in_specs=[row(D_MODEL), row(DA_V_W), row(RET_V_W), row(SC_WIDTH), row(N_BRANCH * D_MODEL)]
                 + [_resident(a.shape) for a in (wda, wret, wsc, wout, ng, wg, wu, cw, cb, wo)],
        out_specs=row(D_MODEL),
        out_shape=jax.ShapeDtypeStruct((t, D_MODEL), F32),
        scratch_shapes=[pltpu.VMEM((tm, D_MODEL), BF16),
                        pltpu.VMEM((tm + SUBLANES, FF_CHUNK), F32),
                        pltpu.VMEM((N_FF_CHUNKS, SUBLANES, FF_CHUNK), F32)],
        compiler_params=pltpu.CompilerParams(dimension_semantics=("arbitrary",),
                                             vmem_limit_bytes=VMEM_LIMIT),
        name="merge_ffn",
    )(x2d, oda, oret, osc, gates, wda, wret, wsc, wout, ng, wg, wu, cw, cb, wo)


def _rel_bucket(n):
    max_exact = REL_BUCKETS // 2
    nf = jnp.maximum(n, 1).astype(F32)
    large = max_exact + (jnp.log(nf / max_exact) / math.log(REL_MAX_DIST / max_exact)
                         * (REL_BUCKETS - max_exact)).astype(jnp.int32)
    large = jnp.minimum(large, REL_BUCKETS - 1)
    return jnp.where(n < max_exact, n, large)


def _near_bias_tiles(rel_bias, tq):
    assert tq >= REL_MAX_DIST
    n_heads = rel_bias.shape[1]
    dist = jnp.arange(2 * tq)
    shifted = (rel_bias[_rel_bucket(dist)] - rel_bias[REL_BUCKETS - 1]) * LOG2E
    by_offset = jnp.concatenate([jnp.full((tq - 1, n_heads), NEG, F32), shifted], axis=0).T
    m = by_offset.shape[1]
    hankel = jnp.tile(by_offset, (1, tq + 1))[:, :tq * (m + 1)].reshape(n_heads, tq, m + 1)
    flipped = hankel[:, ::-1, :]
    diag = flipped[:, :, 0:tq]
    prev = flipped[:, :, tq:2 * tq]
    tiles = jnp.stack([diag, prev], axis=1)
    return jnp.concatenate([tiles, tiles], axis=-1).astype(F32)


def _rotary_tables(seq):
    half = RET_QK_DIM // 2
    inv = ROPE_THETA ** (-jnp.arange(half, dtype=F32) / half)
    ang = jnp.arange(seq).astype(F32)[:, None] * inv[None, :]
    cos, sin = jnp.cos(ang), jnp.sin(ang)
    zero = jnp.zeros_like(sin)
    reps = LANES // RET_QK_DIM
    cos_t = jnp.tile(jnp.concatenate([cos, cos], axis=1), (1, reps))
    sina_t = jnp.tile(jnp.concatenate([-sin, zero], axis=1), (1, reps))
    sinb_t = jnp.tile(jnp.concatenate([zero, sin], axis=1), (1, reps))
    return cos_t, sina_t, sinb_t


def _ff_chunks_cols(w):
    return w.reshape(w.shape[0], N_FF_CHUNKS, FF_CHUNK).transpose(1, 0, 2)


@jax.jit
def _forward(x, rel_bias, norm_mix_g, w_in, b_gate, da_q_norm_g, da_k_norm_g, da_lambda,
             da_subln_g, ret_norm_g, sc_conv_w, sc_conv_b, w_branch_da, w_branch_ret,
             w_branch_sc, w_out, norm_ffn_g, w_ffn_in, ffn_conv_w, ffn_conv_b, w_ffn_out):
    b, s, d = x.shape
    depth = w_in.shape[0]
    assert d == D_MODEL and s % ROW_BLOCK == 0 and s % ATT_BLOCK == 0 and s % RET_STEP == 0
    nb = _near_bias_tiles(rel_bias.astype(F32), ATT_BLOCK)
    cos_t, sina_t, sinb_t = _rotary_tables(s)
    group = jnp.arange(256) // 64
    bd = (group[:, None] == group[None, :]).astype(BF16)

    x2d = x.reshape(b * s, d)
    for l in range(depth):
        lam_init = 0.8 - 0.6 * math.exp(-0.3 * l)
        q, k, vt, rq, rk, rv, rg, osc, gates = _in_proj(
            x2d, s, norm_mix_g[l][None, :], w_in[l].astype(BF16), b_gate[l][None, :],
            jnp.tile(da_q_norm_g[l], 256 // DA_QK_DIM)[None, :],
            jnp.tile(da_k_norm_g[l], 256 // DA_QK_DIM)[None, :],
            cos_t, sina_t, sinb_t, sc_conv_w[l], sc_conv_b[l][None, :], bd)
        to3 = lambda a: a.reshape(b, s, a.shape[-1])
        oda = _diff_attn(to3(q), to3(k), vt, nb, da_lambda[l].astype(F32),
                         da_subln_g[l][:, None], lam_init)
        oret = _retention(to3(rq), to3(rk), to3(rv), to3(rg),
                          jnp.tile(ret_norm_g[l], RET_HEADS)[None, :], bd)
        x2d = _merge_ffn(
            x2d, s, oda.reshape(b * s, DA_V_W), oret.reshape(b * s, RET_V_W), osc, gates,
            w_branch_da[l].astype(BF16), w_branch_ret[l].astype(BF16), w_branch_sc[l].astype(BF16),
            w_out[l].astype(BF16), norm_ffn_g[l][None, :],
            _ff_chunks_cols(w_ffn_in[l][:, :D_FF].astype(BF16)),
            _ff_chunks_cols(w_ffn_in[l][:, D_FF:].astype(BF16)),
            ffn_conv_w[l].reshape(CONV_W, N_FF_CHUNKS, FF_CHUNK).transpose(1, 0, 2),
            ffn_conv_b[l].reshape(N_FF_CHUNKS, 1, FF_CHUNK),
            w_ffn_out[l].astype(BF16).reshape(N_FF_CHUNKS, FF_CHUNK, D_MODEL))
    return x2d.reshape(b, s, d)


def kernel(x, rel_bias, norm_mix_g, w_in, b_gate, da_q_norm_g, da_k_norm_g, da_lambda, da_subln_g,
           ret_norm_g, sc_conv_w, sc_conv_b, w_branch_da, w_branch_ret, w_branch_sc, w_out,
           norm_ffn_g, w_ffn_in, ffn_conv_w, ffn_conv_b, w_ffn_out):
    return _forward(x, rel_bias, norm_mix_g, w_in, b_gate, da_q_norm_g, da_k_norm_g, da_lambda,
                    da_subln_g, ret_norm_g, sc_conv_w, sc_conv_b, w_branch_da, w_branch_ret,
                    w_branch_sc, w_out, norm_ffn_g, w_ffn_in, ffn_conv_w, ffn_conv_b, w_ffn_out)
```

```python
import functools
import math

import jax
import jax.numpy as jnp
import numpy as np
from jax import lax
from jax.experimental import pallas as pl
from jax.experimental.pallas import tpu as pltpu

D_MODEL = 1024
DA_HEADS = 4
DA_QK_DIM = 64
DA_V_DIM = 2 * DA_QK_DIM
RET_HEADS = 4
RET_QK_DIM = 64
RET_V_DIM = 64
SC_WIDTH = 256
CONV_W = 3
D_FF = 2816
REL_BUCKETS = 32
REL_MAX_DIST = 128
ROPE_THETA = 10000.0
EPS = 1e-6
N_BRANCH = 3

DA_QK_W = DA_HEADS * 2 * DA_QK_DIM
DA_V_W = DA_HEADS * DA_V_DIM
RET_QK_W = RET_HEADS * RET_QK_DIM
RET_V_W = RET_HEADS * RET_V_DIM

OFF_DA_Q = 0
OFF_DA_K = OFF_DA_Q + DA_QK_W
OFF_DA_V = OFF_DA_K + DA_QK_W
OFF_R_Q = OFF_DA_V + DA_V_W
OFF_R_K = OFF_R_Q + RET_QK_W
OFF_R_V = OFF_R_K + RET_QK_W
OFF_R_G = OFF_R_V + RET_V_W
OFF_SC_B = OFF_R_G + RET_V_W
OFF_SC_C = OFF_SC_B + SC_WIDTH
OFF_SC_X = OFF_SC_C + SC_WIDTH
OFF_GATE = OFF_SC_X + SC_WIDTH
IN_WIDTH = OFF_GATE + N_BRANCH * D_MODEL

LANES = 128
SUBLANES = 8
ROW_BLOCK = 512
ATT_BLOCK = 512
RET_CHUNK = 128
RET_STEP = 512
GATE_CHUNK = 512
FF_CHUNK = 256
N_FF_CHUNKS = D_FF // FF_CHUNK
VMEM_LIMIT = 56 * 1024 * 1024
NEG = -1e30
LOG2E = math.log2(math.e)
BF16 = jnp.bfloat16
F32 = jnp.float32


def _resident(shape):
    nd = len(shape)
    return pl.BlockSpec(shape, lambda *_: (0,) * nd, pipeline_mode=pl.Buffered(1))


def _group_mean_sq(y, bd):
    return jnp.dot((y * y).astype(BF16), bd, preferred_element_type=F32) * (1.0 / 64.0)


def _sigmoid(z):
    return 1.0 / (1.0 + jnp.exp(-z))


def _in_proj_kernel(x_ref, g_ref, w_ref, bg_ref, qg_ref, kg_ref, cos_ref, sina_ref, sinb_ref,
                    cw_ref, cb_ref, bd_ref,
                    q_out, k_out, vt_out, rq_out, rk_out, rv_out, rg_out, sc_out, gate_out,
                    h_ref, ext_ref, *, blocks_per_seq):
    tm = x_ref.shape[0]
    x = x_ref[...]
    ms = jnp.mean(x * x, axis=-1, keepdims=True)
    h_ref[...] = (x * lax.rsqrt(ms + EPS) * g_ref[...]).astype(BF16)
    bd = bd_ref[...]

    def proj(off, width):
        return jnp.dot(h_ref[...], w_ref[:, off:off + width], preferred_element_type=F32)

    for half in range(DA_QK_W // 256):
        sl = slice(half * 256, (half + 1) * 256)
        y = proj(OFF_DA_Q + half * 256, 256)
        q_out[:, sl] = (y * lax.rsqrt(_group_mean_sq(y, bd) + EPS)
                        * (qg_ref[...] * (DA_QK_DIM ** -0.5 * LOG2E))).astype(BF16)
        y = proj(OFF_DA_K + half * 256, 256)
        k_out[:, sl] = (y * lax.rsqrt(_group_mean_sq(y, bd) + EPS) * kg_ref[...]).astype(BF16)
    v = proj(OFF_DA_V, DA_V_W)
    for c in range(tm // ATT_BLOCK):
        vt_out[0, c] = v[c * ATT_BLOCK:(c + 1) * ATT_BLOCK, :].T.astype(BF16)

    def rotary(y):
        outs = []
        for c in range(RET_QK_W // LANES):
            yc = y[:, c * LANES:(c + 1) * LANES]
            outs.append(yc * cos_ref[...]
                        + pltpu.roll(yc, LANES - 32, axis=1) * sina_ref[...]
                        + pltpu.roll(yc, 32, axis=1) * sinb_ref[...])
        return jnp.concatenate(outs, axis=1)

    rq_out[...] = rotary(proj(OFF_R_Q, RET_QK_W)).astype(BF16)
    rk_out[...] = (rotary(proj(OFF_R_K, RET_QK_W)) * (RET_QK_DIM ** -0.5)).astype(BF16)
    rv_out[...] = proj(OFF_R_V, RET_V_W).astype(BF16)
    rg_out[...] = proj(OFF_R_G, RET_V_W).astype(BF16)

    u = proj(OFF_SC_C, SC_WIDTH) * proj(OFF_SC_X, SC_WIDTH)

    @pl.when(pl.program_id(0) % blocks_per_seq == 0)
    def _():
        ext_ref[0:SUBLANES, :] = jnp.zeros((SUBLANES, SC_WIDTH), F32)

    ext_ref[SUBLANES:SUBLANES + tm, :] = u
    conv = (cb_ref[...]
            + cw_ref[0:1, :] * ext_ref[SUBLANES - 2:SUBLANES - 2 + tm, :]
            + cw_ref[1:2, :] * ext_ref[SUBLANES - 1:SUBLANES - 1 + tm, :]
            + cw_ref[2:3, :] * u)
    ext_ref[0:SUBLANES, :] = u[tm - SUBLANES:, :]
    sc_out[...] = (proj(OFF_SC_B, SC_WIDTH) * conv).astype(BF16)

    for c in range(N_BRANCH * D_MODEL // GATE_CHUNK):
        sl = slice(c * GATE_CHUNK, (c + 1) * GATE_CHUNK)
        z = proj(OFF_GATE + c * GATE_CHUNK, GATE_CHUNK) + bg_ref[:, sl]
        gate_out[:, sl] = _sigmoid(z).astype(BF16)


def _in_proj(x2d, seq, g, w, bg, qg, kg, cos_t, sina_t, sinb_t, cw, cb, bd):
    t = x2d.shape[0]
    tm = ROW_BLOCK
    blocks_per_seq = seq // tm
    row = lambda width: pl.BlockSpec((tm, width), lambda i: (i, 0))
    tab = pl.BlockSpec((tm, LANES), lambda i: (i % blocks_per_seq, 0))
    out_widths = (DA_QK_W, DA_QK_W, RET_QK_W, RET_QK_W, RET_V_W, RET_V_W, SC_WIDTH,
                  N_BRANCH * D_MODEL)
    out_specs = [row(wd) for wd in out_widths]
    out_shape = [jax.ShapeDtypeStruct((t, wd), BF16) for wd in out_widths]
    out_specs.insert(2, pl.BlockSpec((1, tm // ATT_BLOCK, DA_V_W, ATT_BLOCK),
                                     lambda i: (i // blocks_per_seq, i % blocks_per_seq, 0, 0)))
    out_shape.insert(2, jax.ShapeDtypeStruct((t // seq, seq // ATT_BLOCK, DA_V_W, ATT_BLOCK), BF16))
    return pl.pallas_call(
        functools.partial(_in_proj_kernel, blocks_per_seq=blocks_per_seq),
        grid=(t // tm,),
        in_specs=[row(D_MODEL), _resident(g.shape), _resident(w.shape), _resident(bg.shape),
                  _resident(qg.shape), _resident(kg.shape), tab, tab, tab,
                  _resident(cw.shape), _resident(cb.shape), _resident(bd.shape)],
        out_specs=out_specs,
        out_shape=out_shape,
        scratch_shapes=[pltpu.VMEM((tm, D_MODEL), BF16),
                        pltpu.VMEM((tm + SUBLANES, SC_WIDTH), F32)],
        compiler_params=pltpu.CompilerParams(dimension_semantics=("arbitrary",),
                                             vmem_limit_bytes=VMEM_LIMIT),
        name="in_proj",
    )(x2d, g, w, bg, qg, kg, cos_t, sina_t, sinb_t, cw, cb, bd)


def _diff_attn_kernel(q_ref, k_ref, vt_ref, nb_ref, lam_ref, sg_ref, o_ref,
                      s0_ref, s1_ref, acc0_ref, acc1_ref, *, lam_init):
    tq = q_ref.shape[1]
    qi = pl.program_id(2)
    q = q_ref[0]
    lane = lax.broadcasted_iota(jnp.int32, q.shape, 1)
    zero = jnp.zeros_like(q)
    q_map = (jnp.where(lane < DA_QK_DIM, q, zero), jnp.where(lane >= DA_QK_DIM, q, zero))
    s_refs = (s0_ref, s1_ref)
    acc_refs = (acc0_ref, acc1_ref)

    def put_scores(mp, j):
        start = pl.multiple_of(j * tq, tq)
        s = lax.dot_general(k_ref[0, pl.ds(start, tq), :], q_map[mp], (((1,), (1,)), ((), ())),
                            preferred_element_type=F32)
        s_refs[mp][...] = s + nb_ref[0, jnp.minimum(qi - j, 2)]

    def update(mp, j, stats):
        m, l = stats
        m_new = jnp.maximum(m, jnp.max(s_refs[mp][...], axis=0, keepdims=True))
        alpha = jnp.exp2(m - m_new)
        p = jnp.exp2(s_refs[mp][...] - m_new)
        l = alpha * l + jnp.sum(p, axis=0, keepdims=True)
        acc_refs[mp][...] = alpha * acc_refs[mp][...] + jnp.dot(
            vt_ref[0, j], p.astype(BF16), preferred_element_type=F32)
        return m_new, l

    def step(j, stats):
        put_scores(1, j)
        stats0 = update(0, j, stats[0])
        put_scores(0, j + 1)
        stats1 = update(1, j, stats[1])
        return stats0, stats1

    init = (jnp.full((1, tq), NEG, F32), jnp.zeros((1, tq), F32))
    acc0_ref[...] = jnp.zeros_like(acc0_ref)
    acc1_ref[...] = jnp.zeros_like(acc1_ref)
    put_scores(0, 0)
    stats = lax.fori_loop(0, qi, step, (init, init))
    put_scores(1, qi)
    (_, l0), (_, l1) = update(0, qi, stats[0]), update(1, qi, stats[1])

    lp = lam_ref[...]
    lam = (jnp.exp(jnp.sum(lp[0:1] * lp[1:2], axis=1, keepdims=True))
           - jnp.exp(jnp.sum(lp[2:3] * lp[3:4], axis=1, keepdims=True)) + lam_init)
    o = acc0_ref[...] * (1.0 / l0) - lam * (acc1_ref[...] * (1.0 / l1))
    ms = jnp.mean(o * o, axis=0, keepdims=True)
    o_ref[0] = (o * lax.rsqrt(ms + EPS) * (sg_ref[...] * (1.0 - lam_init))).T.astype(BF16)


def _diff_attn(q, k, vt, nb, lam_p, sg, lam_init):
    b, s, _ = q.shape
    tq = ATT_BLOCK
    return pl.pallas_call(
        functools.partial(_diff_attn_kernel, lam_init=lam_init),
        grid=(b, DA_HEADS, s // tq),
        in_specs=[pl.BlockSpec((1, tq, LANES), lambda bi, h, i: (bi, i, h)),
                  pl.BlockSpec((1, s, LANES), lambda bi, h, i: (bi, 0, h)),
                  pl.BlockSpec((1, s // tq, DA_V_DIM, tq), lambda bi, h, i: (bi, 0, h, 0)),
                  pl.BlockSpec((1, 3, tq, tq), lambda bi, h, i: (h, 0, 0, 0)),
                  pl.BlockSpec(lam_p.shape, lambda bi, h, i: (0, 0)),
                  pl.BlockSpec(sg.shape, lambda bi, h, i: (0, 0))],
        out_specs=pl.BlockSpec((1, tq, LANES), lambda bi, h, i: (bi, i, h)),
        out_shape=jax.ShapeDtypeStruct((b, s, DA_V_W), BF16),
        scratch_shapes=[pltpu.VMEM((tq, tq), F32), pltpu.VMEM((tq, tq), F32),
                        pltpu.VMEM((DA_V_DIM, tq), F32), pltpu.VMEM((DA_V_DIM, tq), F32)],
        compiler_params=pltpu.CompilerParams(
            dimension_semantics=("arbitrary", "arbitrary", "arbitrary"),
            vmem_limit_bytes=VMEM_LIMIT),
        name="diff_attn",
    )(q, k, vt, nb, lam_p, sg)


def _ret_log_gamma(head):
    lg = [math.log(1.0 - 2.0 ** (-5.0 - h)) for h in range(RET_HEADS)]
    out = jnp.full(head.shape, lg[RET_HEADS - 1], F32)
    for h in range(RET_HEADS - 2, -1, -1):
        out = jnp.where(head == h, lg[h], out)
    return out


def _retention_kernel(q_ref, k_ref, v_ref, gate_ref, ng_ref, bd_ref, o_ref, state_ref):
    c = RET_CHUNK
    w = RET_QK_W
    n_chunks = q_ref.shape[1] // c

    @pl.when(pl.program_id(1) == 0)
    def _():
        state_ref[...] = jnp.zeros_like(state_ref)

    head_shift = RET_QK_DIM.bit_length() - 1
    chunk_shift = c.bit_length() - 1
    assert 1 << head_shift == RET_QK_DIM == RET_V_DIM and 1 << chunk_shift == c
    lane_head = lax.broadcasted_iota(jnp.int32, (c, w), 1) >> head_shift
    row = lax.broadcasted_iota(jnp.int32, (c, w), 0).astype(F32)
    lg_lane = _ret_log_gamma(lane_head)
    q_decay = jnp.exp(lg_lane * (row + 1.0))
    k_decay = jnp.exp(lg_lane * (c - 1.0 - row))
    chunk_decay = jnp.exp(lg_lane[0:1, :] * float(c))
    srow = lax.broadcasted_iota(jnp.int32, (RET_HEADS * c, c), 0)
    scol = lax.broadcasted_iota(jnp.int32, (RET_HEADS * c, c), 1)
    diff = ((srow & (c - 1)) - scol).astype(F32)
    d_intra = jnp.where(diff >= 0,
                        jnp.exp(_ret_log_gamma(srow >> chunk_shift) * jnp.maximum(diff, 0.0)), 0.0)
    st_row_head = lax.broadcasted_iota(jnp.int32, (w, w), 0) >> head_shift
    st_col_head = lax.broadcasted_iota(jnp.int32, (w, w), 1) >> head_shift
    same_head = st_row_head == st_col_head

    for n in range(n_chunks):
        sl = slice(n * c, (n + 1) * c)
        q = q_ref[0, sl, :]
        k = k_ref[0, sl, :]
        v = v_ref[0, sl, :]
        zero = jnp.zeros_like(q)
        qs = jnp.concatenate([jnp.where(lane_head == h, q, zero) for h in range(RET_HEADS)], axis=0)
        s = lax.dot_general(qs, k, (((1,), (1,)), ((), ())), preferred_element_type=F32)
        o_all = jnp.dot((s * d_intra).astype(BF16), v, preferred_element_type=F32)
        inner = jnp.zeros((c, w), F32)
        for h in range(RET_HEADS):
            inner = jnp.where(lane_head == h, o_all[h * c:(h + 1) * c], inner)
        state = state_ref[...]
        cross = jnp.dot(q, state.astype(BF16), preferred_element_type=F32) * q_decay
        o = inner + cross
        k_dec = (k.astype(F32) * k_decay).T.astype(BF16)
        kv = jnp.dot(k_dec, v, preferred_element_type=F32)
        state_ref[...] = state * chunk_decay + jnp.where(same_head, kv, 0.0)

        y = o * lax.rsqrt(_group_mean_sq(o, bd_ref[...]) + EPS) * ng_ref[...]
        gate = gate_ref[0, sl, :].astype(F32)
        o_ref[0, sl, :] = (y * (gate * _sigmoid(gate))).astype(BF16)


def _retention(rq, rk, rv, rg, ng, bd):
    b, s, w = rq.shape
    ts = RET_STEP
    blk = pl.BlockSpec((1, ts, w), lambda bi, i: (bi, i, 0))
    return pl.pallas_call(
        _retention_kernel,
        grid=(b, s // ts),
        in_specs=[blk, blk, blk, blk, _resident(ng.shape), _resident(bd.shape)],
        out_specs=blk,
        out_shape=jax.ShapeDtypeStruct((b, s, w), BF16),
        scratch_shapes=[pltpu.VMEM((w, w), F32)],
        compiler_params=pltpu.CompilerParams(dimension_semantics=("arbitrary", "arbitrary"),
                                             vmem_limit_bytes=VMEM_LIMIT),
        name="retention",
    )(rq, rk, rv, rg, ng, bd)


def _merge_ffn_kernel(x_ref, oda_ref, oret_ref, osc_ref, gate_ref, wda_ref, wret_ref, wsc_ref,
                      wout_ref, ng_ref, wg_ref, wu_ref, cw_ref, cb_ref, wo_ref,
                      out_ref, h_ref, ext_ref, carry_ref, *, blocks_per_seq):
    tm = x_ref.shape[0]
    d = D_MODEL
    y = (gate_ref[:, 0:d].astype(F32)
         * jnp.dot(oda_ref[...], wda_ref[...], preferred_element_type=F32))
    y += (gate_ref[:, d:2 * d].astype(F32)
          * jnp.dot(oret_ref[...], wret_ref[...], preferred_element_type=F32))
    y += (gate_ref[:, 2 * d:3 * d].astype(F32)
          * jnp.dot(osc_ref[...], wsc_ref[...], preferred_element_type=F32))
    x1 = x_ref[...] + jnp.dot(y.astype(BF16), wout_ref[...], preferred_element_type=F32)
    out_ref[...] = x1
    ms = jnp.mean(x1 * x1, axis=-1, keepdims=True)
    h_ref[...] = (x1 * lax.rsqrt(ms + EPS) * ng_ref[...]).astype(BF16)

    @pl.when(pl.program_id(0) % blocks_per_seq == 0)
    def _():
        carry_ref[...] = jnp.zeros_like(carry_ref)

    def chunk(c, _):
        g = jnp.dot(h_ref[...], wg_ref[c], preferred_element_type=F32)
        u = jnp.dot(h_ref[...], wu_ref[c], preferred_element_type=F32)
        ext_ref[0:SUBLANES, :] = carry_ref[c]
        ext_ref[SUBLANES:SUBLANES + tm, :] = g
        cw = cw_ref[c]
        gc = (cb_ref[c]
              + cw[0:1, :] * ext_ref[SUBLANES - 2:SUBLANES - 2 + tm, :]
              + cw[1:2, :] * ext_ref[SUBLANES - 1:SUBLANES - 1 + tm, :]
              + cw[2:3, :] * g)
        carry_ref[c] = g[tm - SUBLANES:, :]
        a = (gc * _sigmoid(gc) * u).astype(BF16)
        out_ref[...] += jnp.dot(a, wo_ref[c], preferred_element_type=F32)
        return 0

    lax.fori_loop(0, N_FF_CHUNKS, chunk, 0)


def _merge_ffn(x2d, seq, oda, oret, osc, gates, wda, wret, wsc, wout, ng, wg, wu, cw, cb, wo):
    t = x2d.shape[0]
    tm = ROW_BLOCK
    blocks_per_seq = seq // tm
    row = lambda width: pl.BlockSpec((tm, width), lambda i: (i, 0))
    return pl.pallas_call(
        functools.partial(_merge_ffn_kernel, blocks_per_seq=blocks_per_seq),
        grid=(t // tm,),
        in_specs=[row(D_MODEL), row(DA_V_W), row(RET_V_W), row(SC_WIDTH), row(N_BRANCH * D_MODEL)]
                 + [_resident(a.shape) for a in (wda, wret, wsc, wout, ng, wg, wu, cw, cb, wo)],
        out_specs=row(D_MODEL),
        out_shape=jax.ShapeDtypeStruct((t, D_MODEL), F32),
        scratch_shapes=[pltpu.VMEM((tm, D_MODEL), BF16),
                        pltpu.VMEM((tm + SUBLANES, FF_CHUNK), F32),
                        pltpu.VMEM((N_FF_CHUNKS, SUBLANES, FF_CHUNK), F32)],
        compiler_params=pltpu.CompilerParams(dimension_semantics=("arbitrary",),
                                             vmem_limit_bytes=VMEM_LIMIT),
        name="merge_ffn",
    )(x2d, oda, oret, osc, gates, wda, wret, wsc, wout, ng, wg, wu, cw, cb, wo)


def _rel_bucket(n):
    max_exact = REL_BUCKETS // 2
    nf = jnp.maximum(n, 1).astype(F32)
    large = max_exact + (jnp.log(nf / max_exact) / math.log(REL_MAX_DIST / max_exact)
                         * (REL_BUCKETS - max_exact)).astype(jnp.int32)
    large = jnp.minimum(large, REL_BUCKETS - 1)
    return jnp.where(n < max_exact, n, large)


def _near_bias_tiles(rel_bias, tq):
    assert tq >= REL_MAX_DIST
    n_heads = rel_bias.shape[1]
    dist = jnp.arange(2 * tq)
    shifted = (rel_bias[_rel_bucket(dist)] - rel_bias[REL_BUCKETS - 1]) * LOG2E
    by_offset = jnp.concatenate([jnp.full((tq - 1, n_heads), NEG, F32), shifted], axis=0).T
    m = by_offset.shape[1]
    hankel = jnp.tile(by_offset, (1, tq + 1))[:, :tq * (m + 1)].reshape(n_heads, tq, m + 1)
    flipped = hankel[:, ::-1, :]
    diag = flipped[:, :, 0:tq]
    prev = flipped[:, :, tq:2 * tq]
    return jnp.stack([diag, prev, jnp.zeros_like(diag)], axis=1).astype(F32)


def _rotary_tables(seq):
    half = RET_QK_DIM // 2
    inv = ROPE_THETA ** (-jnp.arange(half, dtype=F32) / half)
    ang = jnp.arange(seq).astype(F32)[:, None] * inv[None, :]
    cos, sin = jnp.cos(ang), jnp.sin(ang)
    zero = jnp.zeros_like(sin)
    reps = LANES // RET_QK_DIM
    cos_t = jnp.tile(jnp.concatenate([cos, cos], axis=1), (1, reps))
    sina_t = jnp.tile(jnp.concatenate([-sin, zero], axis=1), (1, reps))
    sinb_t = jnp.tile(jnp.concatenate([zero, sin], axis=1), (1, reps))
    return cos_t, sina_t, sinb_t


def _ff_chunks_cols(w):
    return w.reshape(w.shape[0], N_FF_CHUNKS, FF_CHUNK).transpose(1, 0, 2)


@jax.jit
def _forward(x, rel_bias, norm_mix_g, w_in, b_gate, da_q_norm_g, da_k_norm_g, da_lambda,
             da_subln_g, ret_norm_g, sc_conv_w, sc_conv_b, w_branch_da, w_branch_ret,
             w_branch_sc, w_out, norm_ffn_g, w_ffn_in, ffn_conv_w, ffn_conv_b, w_ffn_out):
    b, s, d = x.shape
    depth = w_in.shape[0]
    assert d == D_MODEL and s % ROW_BLOCK == 0 and s % ATT_BLOCK == 0 and s % RET_STEP == 0
    nb = _near_bias_tiles(rel_bias.astype(F32), ATT_BLOCK)
    cos_t, sina_t, sinb_t = _rotary_tables(s)
    group = jnp.arange(256) // 64
    bd = (group[:, None] == group[None, :]).astype(BF16)

    x2d = x.reshape(b * s, d)
    for l in range(depth):
        lam_init = 0.8 - 0.6 * math.exp(-0.3 * l)
        q, k, vt, rq, rk, rv, rg, osc, gates = _in_proj(
            x2d, s, norm_mix_g[l][None, :], w_in[l].astype(BF16), b_gate[l][None, :],
            jnp.tile(da_q_norm_g[l], 256 // DA_QK_DIM)[None, :],
            jnp.tile(da_k_norm_g[l], 256 // DA_QK_DIM)[None, :],
            cos_t, sina_t, sinb_t, sc_conv_w[l], sc_conv_b[l][None, :], bd)
        to3 = lambda a: a.reshape(b, s, a.shape[-1])
        oda = _diff_attn(to3(q), to3(k), vt, nb, da_lambda[l].astype(F32),
                         da_subln_g[l][:, None], lam_init)
        oret = _retention(to3(rq), to3(rk), to3(rv), to3(rg),
                          jnp.tile(ret_norm_g[l], RET_HEADS)[None, :], bd)
        x2d = _merge_ffn(
            x2d, s, oda.reshape(b * s, DA_V_W), oret.reshape(b * s, RET_V_W), osc, gates,
            w_branch_da[l].astype(BF16), w_branch_ret[l].astype(BF16), w_branch_sc[l].astype(BF16),
            w_out[l].astype(BF16), norm_ffn_g[l][None, :],
            _ff_chunks_cols(w_ffn_in[l][:, :D_FF].astype(BF16)),
            _ff_chunks_cols(w_ffn_in[l][:, D_FF:].astype(BF16)),
            ffn_conv_w[l].reshape(CONV_W, N_FF_CHUNKS, FF_CHUNK).transpose(1, 0, 2),
            ffn_conv_b[l].reshape(N_FF_CHUNKS, 1, FF_CHUNK),
            w_ffn_out[l].astype(BF16).reshape(N_FF_CHUNKS, FF_CHUNK, D_MODEL))
    return x2d.reshape(b, s, d)


def kernel(x, rel_bias, norm_mix_g, w_in, b_gate, da_q_norm_g, da_k_norm_g, da_lambda, da_subln_g,
           ret_norm_g, sc_conv_w, sc_conv_b, w_branch_da, w_branch_ret, w_branch_sc, w_out,
           norm_ffn_g, w_ffn_in, ffn_conv_w, ffn_conv_b, w_ffn_out):
    return _forward(x, rel_bias, norm_mix_g, w_in, b_gate, da_q_norm_g, da_k_norm_g, da_lambda,
                    da_subln_g, ret_norm_g, sc_conv_w, sc_conv_b, w_branch_da, w_branch_ret,
                    w_branch_sc, w_out, norm_ffn_g, w_ffn_in, ffn_conv_w, ffn_conv_b, w_ffn_out)
```

```python
import functools
import math

import jax
import jax.numpy as jnp
import numpy as np
from jax import lax
from jax.experimental import pallas as pl
from jax.experimental.pallas import tpu as pltpu

D_MODEL = 1024
DA_HEADS = 4
DA_QK_DIM = 64
DA_V_DIM = 2 * DA_QK_DIM
RET_HEADS = 4
RET_QK_DIM = 64
RET_V_DIM = 64
SC_WIDTH = 256
CONV_W = 3
D_FF = 2816
REL_BUCKETS = 32
REL_MAX_DIST = 128
ROPE_THETA = 10000.0
EPS = 1e-6
N_BRANCH = 3

DA_QK_W = DA_HEADS * 2 * DA_QK_DIM
DA_V_W = DA_HEADS * DA_V_DIM
RET_QK_W = RET_HEADS * RET_QK_DIM
RET_V_W = RET_HEADS * RET_V_DIM

OFF_DA_Q = 0
OFF_DA_K = OFF_DA_Q + DA_QK_W
OFF_DA_V = OFF_DA_K + DA_QK_W
OFF_R_Q = OFF_DA_V + DA_V_W
OFF_R_K = OFF_R_Q + RET_QK_W
OFF_R_V = OFF_R_K + RET_QK_W
OFF_R_G = OFF_R_V + RET_V_W
OFF_SC_B = OFF_R_G + RET_V_W
OFF_SC_C = OFF_SC_B + SC_WIDTH
OFF_SC_X = OFF_SC_C + SC_WIDTH
OFF_GATE = OFF_SC_X + SC_WIDTH
IN_WIDTH = OFF_GATE + N_BRANCH * D_MODEL

LANES = 128
SUBLANES = 8
ROW_BLOCK = 512
ATT_BLOCK = 512
RET_CHUNK = 128
RET_STEP = 512
GATE_CHUNK = 512
FF_CHUNK = 256
N_FF_CHUNKS = D_FF // FF_CHUNK
VMEM_LIMIT = 56 * 1024 * 1024
NEG = -1e30
LOG2E = math.log2(math.e)
BF16 = jnp.bfloat16
F32 = jnp.float32


def _resident(shape):
    nd = len(shape)
    return pl.BlockSpec(shape, lambda *_: (0,) * nd, pipeline_mode=pl.Buffered(1))


def _group_mean_sq(y, bd):
    return jnp.dot((y * y).astype(BF16), bd, preferred_element_type=F32) * (1.0 / 64.0)


def _sigmoid(z):
    return 1.0 / (1.0 + jnp.exp(-z))


def _in_proj_kernel(x_ref, g_ref, w_ref, bg_ref, qg_ref, kg_ref, cos_ref, sina_ref, sinb_ref,
                    cw_ref, cb_ref, bd_ref,
                    q_out, k_out, vt_out, rq_out, rk_out, rv_out, rg_out, sc_out, gate_out,
                    h_ref, ext_ref, *, blocks_per_seq):
    tm = x_ref.shape[0]

    @pl.when(pl.program_id(0) % blocks_per_seq == 0)
    def _():
        ext_ref[0:SUBLANES, :] = jnp.zeros((SUBLANES, SC_WIDTH), F32)

    x = x_ref[...]
    ms = jnp.mean(x * x, axis=-1, keepdims=True)
    h_ref[...] = (x * lax.rsqrt(ms + EPS) * g_ref[...]).astype(BF16)
    bd = bd_ref[...]

    def proj(off, width):
        return jnp.dot(h_ref[...], w_ref[:, off:off + width], preferred_element_type=F32)

    halves = range(DA_QK_W // 256)
    yq = [proj(OFF_DA_Q + half * 256, 256) for half in halves]
    yk = [proj(OFF_DA_K + half * 256, 256) for half in halves]
    v = proj(OFF_DA_V, DA_V_W)
    for c in range(tm // ATT_BLOCK):
        vt_out[0, c] = v[c * ATT_BLOCK:(c + 1) * ATT_BLOCK, :].T.astype(BF16)

    def rotary(y):
        outs = []
        for c in range(RET_QK_W // LANES):
            yc = y[:, c * LANES:(c + 1) * LANES]
            outs.append(yc * cos_ref[...]
                        + pltpu.roll(yc, LANES - 32, axis=1) * sina_ref[...]
                        + pltpu.roll(yc, 32, axis=1) * sinb_ref[...])
        return jnp.concatenate(outs, axis=1)

    rq_out[...] = rotary(proj(OFF_R_Q, RET_QK_W)).astype(BF16)
    rk_out[...] = (rotary(proj(OFF_R_K, RET_QK_W)) * (RET_QK_DIM ** -0.5)).astype(BF16)

    for half in halves:
        sl = slice(half * 256, (half + 1) * 256)
        q_out[:, sl] = (yq[half] * lax.rsqrt(_group_mean_sq(yq[half], bd) + EPS)
                        * (qg_ref[...] * (DA_QK_DIM ** -0.5 * LOG2E))).astype(BF16)
        k_out[:, sl] = (yk[half] * lax.rsqrt(_group_mean_sq(yk[half], bd) + EPS)
                        * kg_ref[...]).astype(BF16)

    rv_out[...] = proj(OFF_R_V, RET_V_W).astype(BF16)
    rg_out[...] = proj(OFF_R_G, RET_V_W).astype(BF16)

    u = proj(OFF_SC_C, SC_WIDTH) * proj(OFF_SC_X, SC_WIDTH)
    ext_ref[SUBLANES:SUBLANES + tm, :] = u
    conv = (cb_ref[...]
            + cw_ref[0:1, :] * ext_ref[SUBLANES - 2:SUBLANES - 2 + tm, :]
            + cw_ref[1:2, :] * ext_ref[SUBLANES - 1:SUBLANES - 1 + tm, :]
            + cw_ref[2:3, :] * u)
    ext_ref[0:SUBLANES, :] = u[tm - SUBLANES:, :]
    sc_out[...] = (proj(OFF_SC_B, SC_WIDTH) * conv).astype(BF16)

    for c in range(N_BRANCH * D_MODEL // GATE_CHUNK):
        sl = slice(c * GATE_CHUNK, (c + 1) * GATE_CHUNK)
        z = proj(OFF_GATE + c * GATE_CHUNK, GATE_CHUNK) + bg_ref[:, sl]
        gate_out[:, sl] = _sigmoid(z).astype(BF16)


def _in_proj(x2d, seq, g, w, bg, qg, kg, cos_t, sina_t, sinb_t, cw, cb, bd):
    t = x2d.shape[0]
    tm = ROW_BLOCK
    blocks_per_seq = seq // tm
    row = lambda width: pl.BlockSpec((tm, width), lambda i: (i, 0))
    tab = pl.BlockSpec((tm, LANES), lambda i: (i % blocks_per_seq, 0))
    out_widths = (DA_QK_W, DA_QK_W, RET_QK_W, RET_QK_W, RET_V_W, RET_V_W, SC_WIDTH,
                  N_BRANCH * D_MODEL)
    out_specs = [row(wd) for wd in out_widths]
    out_shape = [jax.ShapeDtypeStruct((t, wd), BF16) for wd in out_widths]
    out_specs.insert(2, pl.BlockSpec((1, tm // ATT_BLOCK, DA_V_W, ATT_BLOCK),
                                     lambda i: (i // blocks_per_seq, i % blocks_per_seq, 0, 0)))
    out_shape.insert(2, jax.ShapeDtypeStruct((t // seq, seq // ATT_BLOCK, DA_V_W, ATT_BLOCK), BF16))
    return pl.pallas_call(
        functools.partial(_in_proj_kernel, blocks_per_seq=blocks_per_seq),
        grid=(t // tm,),
        in_specs=[row(D_MODEL), _resident(g.shape), _resident(w.shape), _resident(bg.shape),
                  _resident(qg.shape), _resident(kg.shape), tab, tab, tab,
                  _resident(cw.shape), _resident(cb.shape), _resident(bd.shape)],
        out_specs=out_specs,
        out_shape=out_shape,
        scratch_shapes=[pltpu.VMEM((tm, D_MODEL), BF16),
                        pltpu.VMEM((tm + SUBLANES, SC_WIDTH), F32)],
        compiler_params=pltpu.CompilerParams(dimension_semantics=("arbitrary",),
                                             vmem_limit_bytes=VMEM_LIMIT),
        name="in_proj",
    )(x2d, g, w, bg, qg, kg, cos_t, sina_t, sinb_t, cw, cb, bd)


def _diff_attn_kernel(q_ref, k_ref, vt_ref, nb_ref, lam_ref, sg_ref, o_ref,
                      s0_ref, s1_ref, acc0_ref, acc1_ref, *, lam_init):
    tq = q_ref.shape[1]
    qi = pl.program_id(2)
    q = q_ref[0]
    lane = lax.broadcasted_iota(jnp.int32, q.shape, 1)
    zero = jnp.zeros_like(q)
    q_map = (jnp.where(lane < DA_QK_DIM, q, zero), jnp.where(lane >= DA_QK_DIM, q, zero))
    s_refs = (s0_ref, s1_ref)
    acc_refs = (acc0_ref, acc1_ref)

    def put_scores(mp, j):
        start = pl.multiple_of(j * tq, tq)
        s = lax.dot_general(k_ref[0, pl.ds(start, tq), :], q_map[mp], (((1,), (1,)), ((), ())),
                            preferred_element_type=F32)
        s_refs[mp][...] = s + nb_ref[0, jnp.minimum(qi - j, 2)]

    def update(mp, j, stats):
        m, l = stats
        m_new = jnp.maximum(m, jnp.max(s_refs[mp][...], axis=0, keepdims=True))
        alpha = jnp.exp2(m - m_new)
        p = jnp.exp2(s_refs[mp][...] - m_new)
        l = alpha * l + jnp.sum(p, axis=0, keepdims=True)
        acc_refs[mp][...] = alpha * acc_refs[mp][...] + jnp.dot(
            vt_ref[0, j], p.astype(BF16), preferred_element_type=F32)
        return m_new, l

    def step(j, stats):
        put_scores(1, j)
        stats0 = update(0, j, stats[0])
        put_scores(0, j + 1)
        stats1 = update(1, j, stats[1])
        return stats0, stats1

    init = (jnp.full((1, tq), NEG, F32), jnp.zeros((1, tq), F32))
    acc0_ref[...] = jnp.zeros_like(acc0_ref)
    acc1_ref[...] = jnp.zeros_like(acc1_ref)
    put_scores(0, 0)
    stats = lax.fori_loop(0, qi, step, (init, init))
    put_scores(1, qi)
    (_, l0), (_, l1) = update(0, qi, stats[0]), update(1, qi, stats[1])

    lp = lam_ref[...]
    lam = (jnp.exp(jnp.sum(lp[0:1] * lp[1:2], axis=1, keepdims=True))
           - jnp.exp(jnp.sum(lp[2:3] * lp[3:4], axis=1, keepdims=True)) + lam_init)
    o = acc0_ref[...] * (1.0 / l0) - lam * (acc1_ref[...] * (1.0 / l1))
    ms = jnp.mean(o * o, axis=0, keepdims=True)
    o_ref[0] = (o * lax.rsqrt(ms + EPS) * (sg_ref[...] * (1.0 - lam_init))).T.astype(BF16)


def _diff_attn(q, k, vt, nb, lam_p, sg, lam_init):
    b, s, _ = q.shape
    tq = ATT_BLOCK
    return pl.pallas_call(
        functools.partial(_diff_attn_kernel, lam_init=lam_init),
        grid=(b, DA_HEADS, s // tq),
        in_specs=[pl.BlockSpec((1, tq, LANES), lambda bi, h, i: (bi, i, h)),
                  pl.BlockSpec((1, s, LANES), lambda bi, h, i: (bi, 0, h)),
                  pl.BlockSpec((1, s // tq, DA_V_DIM, tq), lambda bi, h, i: (bi, 0, h, 0)),
                  pl.BlockSpec((1, 3, tq, tq), lambda bi, h, i: (h, 0, 0, 0)),
                  pl.BlockSpec(lam_p.shape, lambda bi, h, i: (0, 0)),
                  pl.BlockSpec(sg.shape, lambda bi, h, i: (0, 0))],
        out_specs=pl.BlockSpec((1, tq, LANES), lambda bi, h, i: (bi, i, h)),
        out_shape=jax.ShapeDtypeStruct((b, s, DA_V_W), BF16),
        scratch_shapes=[pltpu.VMEM((tq, tq), F32), pltpu.VMEM((tq, tq), F32),
                        pltpu.VMEM((DA_V_DIM, tq), F32), pltpu.VMEM((DA_V_DIM, tq), F32)],
        compiler_params=pltpu.CompilerParams(
            dimension_semantics=("arbitrary", "arbitrary", "arbitrary"),
            vmem_limit_bytes=VMEM_LIMIT),
        name="diff_attn",
    )(q, k, vt, nb, lam_p, sg)


def _ret_log_gamma(head):
    lg = [math.log(1.0 - 2.0 ** (-5.0 - h)) for h in range(RET_HEADS)]
    out = jnp.full(head.shape, lg[RET_HEADS - 1], F32)
    for h in range(RET_HEADS - 2, -1, -1):
        out = jnp.where(head == h, lg[h], out)
    return out


def _retention_kernel(q_ref, k_ref, v_ref, gate_ref, ng_ref, bd_ref, o_ref, state_ref):
    c = RET_CHUNK
    w = RET_QK_W
    n_chunks = q_ref.shape[1] // c

    @pl.when(pl.program_id(1) == 0)
    def _():
        state_ref[...] = jnp.zeros_like(state_ref)

    head_shift = RET_QK_DIM.bit_length() - 1
    chunk_shift = c.bit_length() - 1
    assert 1 << head_shift == RET_QK_DIM == RET_V_DIM and 1 << chunk_shift == c
    lane_head = lax.broadcasted_iota(jnp.int32, (c, w), 1) >> head_shift
    row = lax.broadcasted_iota(jnp.int32, (c, w), 0).astype(F32)
    lg_lane = _ret_log_gamma(lane_head)
    q_decay = jnp.exp(lg_lane * (row + 1.0))
    k_decay = jnp.exp(lg_lane * (c - 1.0 - row))
    chunk_decay = jnp.exp(lg_lane[0:1, :] * float(c))
    srow = lax.broadcasted_iota(jnp.int32, (RET_HEADS * c, c), 0)
    scol = lax.broadcasted_iota(jnp.int32, (RET_HEADS * c, c), 1)
    diff = ((srow & (c - 1)) - scol).astype(F32)
    d_intra = jnp.where(diff >= 0,
                        jnp.exp(_ret_log_gamma(srow >> chunk_shift) * jnp.maximum(diff, 0.0)), 0.0)
    st_row_head = lax.broadcasted_iota(jnp.int32, (w, w), 0) >> head_shift
    st_col_head = lax.broadcasted_iota(jnp.int32, (w, w), 1) >> head_shift
    same_head = st_row_head == st_col_head

    for n in range(n_chunks):
        sl = slice(n * c, (n + 1) * c)
        q = q_ref[0, sl, :]
        k = k_ref[0, sl, :]
        v = v_ref[0, sl, :]
        zero = jnp.zeros_like(q)
        qs = jnp.concatenate([jnp.where(lane_head == h, q, zero) for h in range(RET_HEADS)], axis=0)
        s = lax.dot_general(qs, k, (((1,), (1,)), ((), ())), preferred_element_type=F32)
        o_all = jnp.dot((s * d_intra).astype(BF16), v, preferred_element_type=F32)
        inner = jnp.zeros((c, w), F32)
        for h in range(RET_HEADS):
            inner = jnp.where(lane_head == h, o_all[h * c:(h + 1) * c], inner)
        state = state_ref[...]
        cross = jnp.dot(q, state.astype(BF16), preferred_element_type=F32) * q_decay
        o = inner + cross
        k_dec = (k.astype(F32) * k_decay).T.astype(BF16)
        kv = jnp.dot(k_dec, v, preferred_element_type=F32)
        state_ref[...] = state * chunk_decay + jnp.where(same_head, kv, 0.0)

        y = o * lax.rsqrt(_group_mean_sq(o, bd_ref[...]) + EPS) * ng_ref[...]
        gate = gate_ref[0, sl, :].astype(F32)
        o_ref[0, sl, :] = (y * (gate * _sigmoid(gate))).astype(BF16)


def _retention(rq, rk, rv, rg, ng, bd):
    b, s, w = rq.shape
    ts = RET_STEP
    blk = pl.BlockSpec((1, ts, w), lambda bi, i: (bi, i, 0))
    return pl.pallas_call(
        _retention_kernel,
        grid=(b, s // ts),
        in_specs=[blk, blk, blk, blk, _resident(ng.shape), _resident(bd.shape)],
        out_specs=blk,
        out_shape=jax.ShapeDtypeStruct((b, s, w), BF16),
        scratch_shapes=[pltpu.VMEM((w, w), F32)],
        compiler_params=pltpu.CompilerParams(dimension_semantics=("arbitrary", "arbitrary"),
                                             vmem_limit_bytes=VMEM_LIMIT),
        name="retention",
    )(rq, rk, rv, rg, ng, bd)


def _merge_ffn_kernel(x_ref, oda_ref, oret_ref, osc_ref, gate_ref, wda_ref, wret_ref, wsc_ref,
                      wout_ref, ng_ref, win_ref, cw_ref, cb_ref, wo_ref,
                      out_ref, h_ref, ext_ref, carry_ref, act_ref, *, blocks_per_seq):
    tm = x_ref.shape[0]
    d = D_MODEL
    y = (gate_ref[:, 0:d].astype(F32)
         * jnp.dot(oda_ref[...], wda_ref[...], preferred_element_type=F32))
    y += (gate_ref[:, d:2 * d].astype(F32)
          * jnp.dot(oret_ref[...], wret_ref[...], preferred_element_type=F32))
    y += (gate_ref[:, 2 * d:3 * d].astype(F32)
          * jnp.dot(osc_ref[...], wsc_ref[...], preferred_element_type=F32))
    x1 = x_ref[...] + jnp.dot(y.astype(BF16), wout_ref[...], preferred_element_type=F32)
    out_ref[...] = x1
    ms = jnp.mean(x1 * x1, axis=-1, keepdims=True)
    h_ref[...] = (x1 * lax.rsqrt(ms + EPS) * ng_ref[...]).astype(BF16)

    @pl.when(pl.program_id(0) % blocks_per_seq == 0)
    def _():
        carry_ref[...] = jnp.zeros_like(carry_ref)

    for c in range(N_FF_CHUNKS):
        sl = slice(c * FF_CHUNK, (c + 1) * FF_CHUNK)
        g = jnp.dot(h_ref[...], win_ref[:, sl], preferred_element_type=F32)
        u = jnp.dot(h_ref[...], win_ref[:, D_FF + c * FF_CHUNK:D_FF + (c + 1) * FF_CHUNK],
                    preferred_element_type=F32)
        ext_ref[0:SUBLANES, :] = carry_ref[:, sl]
        ext_ref[SUBLANES:SUBLANES + tm, :] = g
        gc = (cb_ref[:, sl]
              + cw_ref[0:1, sl] * ext_ref[SUBLANES - 2:SUBLANES - 2 + tm, :]
              + cw_ref[1:2, sl] * ext_ref[SUBLANES - 1:SUBLANES - 1 + tm, :]
              + cw_ref[2:3, sl] * g)
        carry_ref[:, sl] = g[tm - SUBLANES:, :]
        act_ref[:, sl] = (gc * _sigmoid(gc) * u).astype(BF16)

    out_ref[...] += jnp.dot(act_ref[...], wo_ref[...], preferred_element_type=F32)


def _merge_ffn(x2d, seq, oda, oret, osc, gates, wda, wret, wsc, wout, ng, win, cw, cb, wo):
    t = x2d.shape[0]
    tm = ROW_BLOCK
    blocks_per_seq = seq // tm
    row = lambda width: pl.BlockSpec((tm, width), lambda i: (i, 0))
    return pl.pallas_call(
        functools.partial(_merge_ffn_kernel, blocks_per_seq=blocks_per_seq),
        grid=(t // tm,),
        in_specs=[row(D_MODEL), row(DA_V_W), row(RET_V_W), row(SC_WIDTH), row(N_BRANCH * D_MODEL)]
                 + [_resident(a.shape) for a in (wda, wret, wsc, wout, ng, win, cw, cb, wo)],
        out_specs=row(D_MODEL),
        out_shape=jax.ShapeDtypeStruct((t, D_MODEL), F32),
        scratch_shapes=[pltpu.VMEM((tm, D_MODEL), BF16),
                        pltpu.VMEM((tm + SUBLANES, FF_CHUNK), F32),
                        pltpu.VMEM((SUBLANES, D_FF), F32),
                        pltpu.VMEM((tm, D_FF), BF16)],
        compiler_params=pltpu.CompilerParams(dimension_semantics=("arbitrary",),
                                             vmem_limit_bytes=VMEM_LIMIT),
        name="merge_ffn",
    )(x2d, oda, oret, osc, gates, wda, wret, wsc, wout, ng, win, cw, cb, wo)


def _rel_bucket(n):
    max_exact = REL_BUCKETS // 2
    nf = jnp.maximum(n, 1).astype(F32)
    large = max_exact + (jnp.log(nf / max_exact) / math.log(REL_MAX_DIST / max_exact)
                         * (REL_BUCKETS - max_exact)).astype(jnp.int32)
    large = jnp.minimum(large, REL_BUCKETS - 1)
    return jnp.where(n < max_exact, n, large)


def _near_bias_tiles(rel_bias, tq):
    assert tq >= REL_MAX_DIST
    n_heads = rel_bias.shape[1]
    dist = jnp.arange(2 * tq)
    shifted = (rel_bias[_rel_bucket(dist)] - rel_bias[REL_BUCKETS - 1]) * LOG2E
    by_offset = jnp.concatenate([jnp.full((tq - 1, n_heads), NEG, F32), shifted,
                                 jnp.zeros((1, n_heads), F32)], axis=0).T
    m = by_offset.shape[1]
    skew = jnp.tile(by_offset, (1, tq))[:, :tq * (m - 1)].reshape(n_heads, tq, m - 1)
    diag = skew[:, :, tq - 1:2 * tq - 1]
    prev = skew[:, :, 2 * tq - 1:3 * tq - 1]
    return jnp.stack([diag, prev, jnp.zeros_like(diag)], axis=1).astype(F32)


def _rotary_tables(seq):
    half = RET_QK_DIM // 2
    inv = ROPE_THETA ** (-jnp.arange(half, dtype=F32) / half)
    ang = jnp.arange(seq).astype(F32)[:, None] * inv[None, :]
    cos, sin = jnp.cos(ang), jnp.sin(ang)
    zero = jnp.zeros_like(sin)
    reps = LANES // RET_QK_DIM
    cos_t = jnp.tile(jnp.concatenate([cos, cos], axis=1), (1, reps))
    sina_t = jnp.tile(jnp.concatenate([-sin, zero], axis=1), (1, reps))
    sinb_t = jnp.tile(jnp.concatenate([zero, sin], axis=1), (1, reps))
    return cos_t, sina_t, sinb_t


@jax.jit
def _forward(x, rel_bias, norm_mix_g, w_in, b_gate, da_q_norm_g, da_k_norm_g, da_lambda,
             da_subln_g, ret_norm_g, sc_conv_w, sc_conv_b, w_branch_da, w_branch_ret,
             w_branch_sc, w_out, norm_ffn_g, w_ffn_in, ffn_conv_w, ffn_conv_b, w_ffn_out):
    b, s, d = x.shape
    depth = w_in.shape[0]
    assert d == D_MODEL and s % ROW_BLOCK == 0 and s % ATT_BLOCK == 0 and s % RET_STEP == 0
    nb = _near_bias_tiles(rel_bias.astype(F32), ATT_BLOCK)
    cos_t, sina_t, sinb_t = _rotary_tables(s)
    group = jnp.arange(256) // 64
    bd = (group[:, None] == group[None, :]).astype(BF16)

    x2d = x.reshape(b * s, d)
    for l in range(depth):
        lam_init = 0.8 - 0.6 * math.exp(-0.3 * l)
        q, k, vt, rq, rk, rv, rg, osc, gates = _in_proj(
            x2d, s, norm_mix_g[l][None, :], w_in[l].astype(BF16), b_gate[l][None, :],
            jnp.tile(da_q_norm_g[l], 256 // DA_QK_DIM)[None, :],
            jnp.tile(da_k_norm_g[l], 256 // DA_QK_DIM)[None, :],
            cos_t, sina_t, sinb_t, sc_conv_w[l], sc_conv_b[l][None, :], bd)
        to3 = lambda a: a.reshape(b, s, a.shape[-1])
        oda = _diff_attn(to3(q), to3(k), vt, nb, da_lambda[l].astype(F32),
                         da_subln_g[l][:, None], lam_init)
        oret = _retention(to3(rq), to3(rk), to3(rv), to3(rg),
                          jnp.tile(ret_norm_g[l], RET_HEADS)[None, :], bd)
        x2d = _merge_ffn(
            x2d, s, oda.reshape(b * s, DA_V_W), oret.reshape(b * s, RET_V_W), osc, gates,
            w_branch_da[l].astype(BF16), w_branch_ret[l].astype(BF16), w_branch_sc[l].astype(BF16),
            w_out[l].astype(BF16), norm_ffn_g[l][None, :],
            w_ffn_in[l].astype(BF16), ffn_conv_w[l], ffn_conv_b[l][None, :],
            w_ffn_out[l].astype(BF16))
    return x2d.reshape(b, s, d)


def kernel(x, rel_bias, norm_mix_g, w_in, b_gate, da_q_norm_g, da_k_norm_g, da_lambda, da_subln_g,
           ret_norm_g, sc_conv_w, sc_conv_b, w_branch_da, w_branch_ret, w_branch_sc, w_out,
           norm_ffn_g, w_ffn_in, ffn_conv_w, ffn_conv_b, w_ffn_out):
    return _forward(x, rel_bias, norm_mix_g, w_in, b_gate, da_q_norm_g, da_k_norm_g, da_lambda,
                    da_subln_g, ret_norm_g, sc_conv_w, sc_conv_b, w_branch_da, w_branch_ret,
                    w_branch_sc, w_out, norm_ffn_g, w_ffn_in, ffn_conv_w, ffn_conv_b, w_ffn_out)
```

```python
import functools
import math

import jax
import jax.numpy as jnp
import numpy as np
from jax import lax
from jax.experimental import pallas as pl
from jax.experimental.pallas import tpu as pltpu

D_MODEL = 1024
DA_HEADS = 4
DA_QK_DIM = 64
DA_V_DIM = 2 * DA_QK_DIM
RET_HEADS = 4
RET_QK_DIM = 64
RET_V_DIM = 64
SC_WIDTH = 256
CONV_W = 3
D_FF = 2816
REL_BUCKETS = 32
REL_MAX_DIST = 128
ROPE_THETA = 10000.0
EPS = 1e-6
N_BRANCH = 3

DA_QK_W = DA_HEADS * 2 * DA_QK_DIM
DA_V_W = DA_HEADS * DA_V_DIM
RET_QK_W = RET_HEADS * RET_QK_DIM
RET_V_W = RET_HEADS * RET_V_DIM

OFF_DA_Q = 0
OFF_DA_K = OFF_DA_Q + DA_QK_W
OFF_DA_V = OFF_DA_K + DA_QK_W
OFF_R_Q = OFF_DA_V + DA_V_W
OFF_R_K = OFF_R_Q + RET_QK_W
OFF_R_V = OFF_R_K + RET_QK_W
OFF_R_G = OFF_R_V + RET_V_W
OFF_SC_B = OFF_R_G + RET_V_W
OFF_SC_C = OFF_SC_B + SC_WIDTH
OFF_SC_X = OFF_SC_C + SC_WIDTH
OFF_GATE = OFF_SC_X + SC_WIDTH
IN_WIDTH = OFF_GATE + N_BRANCH * D_MODEL

LANES = 128
SUBLANES = 8
ROW_BLOCK = 512
ATT_BLOCK = 512
RET_CHUNK = 128
GATE_CHUNK = 512
FF_CHUNK = 256
N_FF_CHUNKS = D_FF // FF_CHUNK
VMEM_LIMIT = 56 * 1024 * 1024
NEG = -1e30
LOG2E = math.log2(math.e)
BF16 = jnp.bfloat16
F32 = jnp.float32


def _resident(shape):
    nd = len(shape)
    return pl.BlockSpec(shape, lambda *_: (0,) * nd, pipeline_mode=pl.Buffered(1))


def _group_mean_sq(y, bd):
    return jnp.dot((y * y).astype(BF16), bd, preferred_element_type=F32) * (1.0 / 64.0)


def _sigmoid(z):
    return 1.0 / (1.0 + jnp.exp(-z))


def _in_proj_kernel(x_ref, g_ref, w_ref, bg_ref, qg_ref, kg_ref, cos_ref, sina_ref, sinb_ref,
                    cw_ref, cb_ref, bd_ref, rn_ref,
                    q_out, k_out, vt_out, ret_out, sc_out, gate_out,
                    h_ref, ext_ref, state_ref, *, blocks_per_seq):
    tm = x_ref.shape[0]

    @pl.when(pl.program_id(0) % blocks_per_seq == 0)
    def _():
        ext_ref[0:SUBLANES, :] = jnp.zeros((SUBLANES, SC_WIDTH), F32)
        state_ref[...] = jnp.zeros_like(state_ref)

    x = x_ref[...]
    ms = jnp.mean(x * x, axis=-1, keepdims=True)
    h_ref[...] = (x * lax.rsqrt(ms + EPS) * g_ref[...]).astype(BF16)
    bd = bd_ref[...]

    def proj(off, width):
        return jnp.dot(h_ref[...], w_ref[:, off:off + width], preferred_element_type=F32)

    halves = range(DA_QK_W // 256)
    yq = [proj(OFF_DA_Q + half * 256, 256) for half in halves]
    yk = [proj(OFF_DA_K + half * 256, 256) for half in halves]
    v = proj(OFF_DA_V, DA_V_W)
    for c in range(tm // ATT_BLOCK):
        vt_out[0, c] = v[c * ATT_BLOCK:(c + 1) * ATT_BLOCK, :].T.astype(BF16)

    def rotary(y):
        outs = []
        for c in range(RET_QK_W // LANES):
            yc = y[:, c * LANES:(c + 1) * LANES]
            outs.append(yc * cos_ref[...]
                        + pltpu.roll(yc, LANES - 32, axis=1) * sina_ref[...]
                        + pltpu.roll(yc, 32, axis=1) * sinb_ref[...])
        return jnp.concatenate(outs, axis=1)

    rq = rotary(proj(OFF_R_Q, RET_QK_W)).astype(BF16)
    rk = (rotary(proj(OFF_R_K, RET_QK_W)) * (RET_QK_DIM ** -0.5)).astype(BF16)

    for half in halves:
        sl = slice(half * 256, (half + 1) * 256)
        q_out[:, sl] = (yq[half] * lax.rsqrt(_group_mean_sq(yq[half], bd) + EPS)
                        * (qg_ref[...] * (DA_QK_DIM ** -0.5 * LOG2E))).astype(BF16)
        k_out[:, sl] = (yk[half] * lax.rsqrt(_group_mean_sq(yk[half], bd) + EPS)
                        * kg_ref[...]).astype(BF16)

    rv = proj(OFF_R_V, RET_V_W).astype(BF16)
    ret_steps = _retention_steps(rq, rk, rv, proj(OFF_R_G, RET_V_W), rn_ref[...], bd, state_ref,
                                 ret_out)

    u = proj(OFF_SC_C, SC_WIDTH) * proj(OFF_SC_X, SC_WIDTH)
    ext_ref[SUBLANES:SUBLANES + tm, :] = u
    conv = (cb_ref[...]
            + cw_ref[0:1, :] * ext_ref[SUBLANES - 2:SUBLANES - 2 + tm, :]
            + cw_ref[1:2, :] * ext_ref[SUBLANES - 1:SUBLANES - 1 + tm, :]
            + cw_ref[2:3, :] * u)
    ext_ref[0:SUBLANES, :] = u[tm - SUBLANES:, :]
    sc_out[...] = (proj(OFF_SC_B, SC_WIDTH) * conv).astype(BF16)

    for c in range(N_BRANCH * D_MODEL // GATE_CHUNK):
        next(ret_steps, None)
        sl = slice(c * GATE_CHUNK, (c + 1) * GATE_CHUNK)
        z = proj(OFF_GATE + c * GATE_CHUNK, GATE_CHUNK) + bg_ref[:, sl]
        gate_out[:, sl] = _sigmoid(z).astype(BF16)
    for _ in ret_steps:
        pass


def _in_proj(x2d, seq, g, w, bg, qg, kg, cos_t, sina_t, sinb_t, cw, cb, bd, rn):
    t = x2d.shape[0]
    tm = ROW_BLOCK
    blocks_per_seq = seq // tm
    row = lambda width: pl.BlockSpec((tm, width), lambda i: (i, 0))
    tab = pl.BlockSpec((tm, LANES), lambda i: (i % blocks_per_seq, 0))
    out_widths = (DA_QK_W, DA_QK_W, RET_V_W, SC_WIDTH, N_BRANCH * D_MODEL)
    out_specs = [row(wd) for wd in out_widths]
    out_shape = [jax.ShapeDtypeStruct((t, wd), BF16) for wd in out_widths]
    out_specs.insert(2, pl.BlockSpec((1, tm // ATT_BLOCK, DA_V_W, ATT_BLOCK),
                                     lambda i: (i // blocks_per_seq, i % blocks_per_seq, 0, 0)))
    out_shape.insert(2, jax.ShapeDtypeStruct((t // seq, seq // ATT_BLOCK, DA_V_W, ATT_BLOCK), BF16))
    return pl.pallas_call(
        functools.partial(_in_proj_kernel, blocks_per_seq=blocks_per_seq),
        grid=(t // tm,),
        in_specs=[row(D_MODEL), _resident(g.shape), _resident(w.shape), _resident(bg.shape),
                  _resident(qg.shape), _resident(kg.shape), tab, tab, tab,
                  _resident(cw.shape), _resident(cb.shape), _resident(bd.shape),
                  _resident(rn.shape)],
        out_specs=out_specs,
        out_shape=out_shape,
        scratch_shapes=[pltpu.VMEM((tm, D_MODEL), BF16),
                        pltpu.VMEM((tm + SUBLANES, SC_WIDTH), F32),
                        pltpu.VMEM((RET_QK_W, RET_V_W), F32)],
        compiler_params=pltpu.CompilerParams(dimension_semantics=("arbitrary",),
                                             vmem_limit_bytes=VMEM_LIMIT),
        name="in_proj",
    )(x2d, g, w, bg, qg, kg, cos_t, sina_t, sinb_t, cw, cb, bd, rn)


def _diff_attn_kernel(q_ref, k_ref, vt_ref, nb_ref, lam_ref, sg_ref, o_ref,
                      s0_ref, s1_ref, acc0_ref, acc1_ref, *, lam_init):
    tq = q_ref.shape[1]
    qi = pl.program_id(2)
    q = q_ref[0]
    lane = lax.broadcasted_iota(jnp.int32, q.shape, 1)
    zero = jnp.zeros_like(q)
    q_map = (jnp.where(lane < DA_QK_DIM, q, zero), jnp.where(lane >= DA_QK_DIM, q, zero))
    s_refs = (s0_ref, s1_ref)
    acc_refs = (acc0_ref, acc1_ref)

    def put_scores(mp, j, far=False):
        start = pl.multiple_of(j * tq, tq)
        s = lax.dot_general(k_ref[0, pl.ds(start, tq), :], q_map[mp], (((1,), (1,)), ((), ())),
                            preferred_element_type=F32)
        s_refs[mp][...] = s if far else s + nb_ref[0, jnp.minimum(qi - j, 2)]

    def update(mp, j, stats):
        m, l = stats
        m_new = jnp.maximum(m, jnp.max(s_refs[mp][...], axis=0, keepdims=True))
        alpha = jnp.exp2(m - m_new)
        p = jnp.exp2(s_refs[mp][...] - m_new)
        l = alpha * l + jnp.sum(p, axis=0, keepdims=True)
        acc_refs[mp][...] = alpha * acc_refs[mp][...] + jnp.dot(
            vt_ref[0, j], p.astype(BF16), preferred_element_type=F32)
        return m_new, l

    def step(j, stats, far=False):
        put_scores(1, j, far)
        stats0 = update(0, j, stats[0])
        put_scores(0, j + 1, far)
        stats1 = update(1, j, stats[1])
        return stats0, stats1

    init = (jnp.full((1, tq), NEG, F32), jnp.zeros((1, tq), F32))
    acc0_ref[...] = jnp.zeros_like(acc0_ref)
    acc1_ref[...] = jnp.zeros_like(acc1_ref)
    put_scores(0, 0)
    n_far = jnp.maximum(qi - 2, 0)
    stats = lax.fori_loop(0, n_far, functools.partial(step, far=True), (init, init))
    stats = lax.fori_loop(n_far, qi, step, stats)
    put_scores(1, qi)
    (_, l0), (_, l1) = update(0, qi, stats[0]), update(1, qi, stats[1])

    lp = lam_ref[...]
    lam = (jnp.exp(jnp.sum(lp[0:1] * lp[1:2], axis=1, keepdims=True))
           - jnp.exp(jnp.sum(lp[2:3] * lp[3:4], axis=1, keepdims=True)) + lam_init)
    o = acc0_ref[...] * (1.0 / l0) - lam * (acc1_ref[...] * (1.0 / l1))
    ms = jnp.mean(o * o, axis=0, keepdims=True)
    o_ref[0] = (o * lax.rsqrt(ms + EPS) * (sg_ref[...] * (1.0 - lam_init))).T.astype(BF16)


def _diff_attn(q, k, vt, nb, lam_p, sg, lam_init):
    b, s, _ = q.shape
    tq = ATT_BLOCK
    return pl.pallas_call(
        functools.partial(_diff_attn_kernel, lam_init=lam_init),
        grid=(b, DA_HEADS, s // tq),
        in_specs=[pl.BlockSpec((1, tq, LANES), lambda bi, h, i: (bi, i, h)),
                  pl.BlockSpec((1, s, LANES), lambda bi, h, i: (bi, 0, h)),
                  pl.BlockSpec((1, s // tq, DA_V_DIM, tq), lambda bi, h, i: (bi, 0, h, 0)),
                  pl.BlockSpec((1, 3, tq, tq), lambda bi, h, i: (h, 0, 0, 0)),
                  pl.BlockSpec(lam_p.shape, lambda bi, h, i: (0, 0)),
                  pl.BlockSpec(sg.shape, lambda bi, h, i: (0, 0))],
        out_specs=pl.BlockSpec((1, tq, LANES), lambda bi, h, i: (bi, i, h)),
        out_shape=jax.ShapeDtypeStruct((b, s, DA_V_W), BF16),
        scratch_shapes=[pltpu.VMEM((tq, tq), F32), pltpu.VMEM((tq, tq), F32),
                        pltpu.VMEM((DA_V_DIM, tq), F32), pltpu.VMEM((DA_V_DIM, tq), F32)],
        compiler_params=pltpu.CompilerParams(
            dimension_semantics=("arbitrary", "arbitrary", "arbitrary"),
            vmem_limit_bytes=VMEM_LIMIT),
        name="diff_attn",
    )(q, k, vt, nb, lam_p, sg)


def _ret_log_gamma(head):
    lg = [math.log(1.0 - 2.0 ** (-5.0 - h)) for h in range(RET_HEADS)]
    out = jnp.full(head.shape, lg[RET_HEADS - 1], F32)
    for h in range(RET_HEADS - 2, -1, -1):
        out = jnp.where(head == h, lg[h], out)
    return out


def _retention_steps(q, k, v, gate, ng, bd, state_ref, o_ref):
    c = RET_CHUNK
    w = RET_QK_W
    n_chunks = q.shape[0] // c
    head_shift = RET_QK_DIM.bit_length() - 1
    chunk_shift = c.bit_length() - 1
    assert 1 << head_shift == RET_QK_DIM == RET_V_DIM and 1 << chunk_shift == c
    lane_head = lax.broadcasted_iota(jnp.int32, (c, w), 1) >> head_shift
    row = lax.broadcasted_iota(jnp.int32, (c, w), 0).astype(F32)
    lg_lane = _ret_log_gamma(lane_head)
    q_decay = jnp.exp(lg_lane * (row + 1.0))
    k_decay = jnp.exp(lg_lane * (c - 1.0 - row))
    chunk_decay = jnp.exp(lg_lane[0:1, :] * float(c))
    srow = lax.broadcasted_iota(jnp.int32, (RET_HEADS * c, c), 0)
    scol = lax.broadcasted_iota(jnp.int32, (RET_HEADS * c, c), 1)
    diff = ((srow & (c - 1)) - scol).astype(F32)
    d_intra = jnp.where(diff >= 0,
                        jnp.exp(_ret_log_gamma(srow >> chunk_shift) * jnp.maximum(diff, 0.0)), 0.0)
    st_row_head = lax.broadcasted_iota(jnp.int32, (w, w), 0) >> head_shift
    st_col_head = lax.broadcasted_iota(jnp.int32, (w, w), 1) >> head_shift
    same_head = st_row_head == st_col_head

    for n in range(n_chunks):
        sl = slice(n * c, (n + 1) * c)
        qc, kc, vc = q[sl], k[sl], v[sl]
        zero = jnp.zeros_like(qc)
        qs = jnp.concatenate([jnp.where(lane_head == h, qc, zero) for h in range(RET_HEADS)], axis=0)
        s = lax.dot_general(qs, kc, (((1,), (1,)), ((), ())), preferred_element_type=F32)
        state = state_ref[...]
        cross = jnp.dot(qc, state.astype(BF16), preferred_element_type=F32) * q_decay
        k_dec = (kc.astype(F32) * k_decay).T.astype(BF16)
        kv = jnp.dot(k_dec, vc, preferred_element_type=F32)
        state_ref[...] = state * chunk_decay + jnp.where(same_head, kv, 0.0)
        yield
        o_all = jnp.dot((s * d_intra).astype(BF16), vc, preferred_element_type=F32)
        inner = jnp.zeros((c, w), F32)
        for h in range(RET_HEADS):
            inner = jnp.where(lane_head == h, o_all[h * c:(h + 1) * c], inner)
        o = inner + cross

        y = o * lax.rsqrt(_group_mean_sq(o, bd) + EPS) * ng
        g = gate[sl]
        o_ref[sl, :] = (y * (g * _sigmoid(g))).astype(BF16)


def _merge_ffn_kernel(x_ref, oda_ref, oret_ref, osc_ref, gate_ref, wda_ref, wret_ref, wsc_ref,
                      wout_ref, ng_ref, win_ref, cw_ref, cb_ref, wo_ref,
                      out_ref, h_ref, ext_ref, carry_ref, act_ref, *, blocks_per_seq):
    tm = x_ref.shape[0]
    d = D_MODEL
    y = (gate_ref[:, 0:d].astype(F32)
         * jnp.dot(oda_ref[...], wda_ref[...], preferred_element_type=F32))
    y += (gate_ref[:, d:2 * d].astype(F32)
          * jnp.dot(oret_ref[...], wret_ref[...], preferred_element_type=F32))
    y += (gate_ref[:, 2 * d:3 * d].astype(F32)
          * jnp.dot(osc_ref[...], wsc_ref[...], preferred_element_type=F32))
    x1 = x_ref[...] + jnp.dot(y.astype(BF16), wout_ref[...], preferred_element_type=F32)
    out_ref[...] = x1
    ms = jnp.mean(x1 * x1, axis=-1, keepdims=True)
    h_ref[...] = (x1 * lax.rsqrt(ms + EPS) * ng_ref[...]).astype(BF16)

    @pl.when(pl.program_id(0) % blocks_per_seq == 0)
    def _():
        carry_ref[...] = jnp.zeros_like(carry_ref)

    for c in range(N_FF_CHUNKS):
        sl = slice(c * FF_CHUNK, (c + 1) * FF_CHUNK)
        g = jnp.dot(h_ref[...], win_ref[:, sl], preferred_element_type=F32)
        u = jnp.dot(h_ref[...], win_ref[:, D_FF + c * FF_CHUNK:D_FF + (c + 1) * FF_CHUNK],
                    preferred_element_type=F32)
        ext_ref[0:SUBLANES, :] = carry_ref[:, sl]
        ext_ref[SUBLANES:SUBLANES + tm, :] = g
        gc = (cb_ref[:, sl]
              + cw_ref[0:1, sl] * ext_ref[SUBLANES - 2:SUBLANES - 2 + tm, :]
              + cw_ref[1:2, sl] * ext_ref[SUBLANES - 1:SUBLANES - 1 + tm, :]
              + cw_ref[2:3, sl] * g)
        carry_ref[:, sl] = g[tm - SUBLANES:, :]
        act_ref[:, sl] = (gc * _sigmoid(gc) * u).astype(BF16)

    out_ref[...] += jnp.dot(act_ref[...], wo_ref[...], preferred_element_type=F32)


def _merge_ffn(x2d, seq, oda, oret, osc, gates, wda, wret, wsc, wout, ng, win, cw, cb, wo):
    t = x2d.shape[0]
    tm = ROW_BLOCK
    blocks_per_seq = seq // tm
    row = lambda width: pl.BlockSpec((tm, width), lambda i: (i, 0))
    return pl.pallas_call(
        functools.partial(_merge_ffn_kernel, blocks_per_seq=blocks_per_seq),
        grid=(t // tm,),
        in_specs=[row(D_MODEL), row(DA_V_W), row(RET_V_W), row(SC_WIDTH), row(N_BRANCH * D_MODEL)]
                 + [_resident(a.shape) for a in (wda, wret, wsc, wout, ng, win, cw, cb, wo)],
        out_specs=row(D_MODEL),
        out_shape=jax.ShapeDtypeStruct((t, D_MODEL), F32),
        scratch_shapes=[pltpu.VMEM((tm, D_MODEL), BF16),
                        pltpu.VMEM((tm + SUBLANES, FF_CHUNK), F32),
                        pltpu.VMEM((SUBLANES, D_FF), F32),
                        pltpu.VMEM((tm, D_FF), BF16)],
        compiler_params=pltpu.CompilerParams(dimension_semantics=("arbitrary",),
                                             vmem_limit_bytes=VMEM_LIMIT),
        name="merge_ffn",
    )(x2d, oda, oret, osc, gates, wda, wret, wsc, wout, ng, win, cw, cb, wo)


def _rel_bucket(n):
    max_exact = REL_BUCKETS // 2
    nf = jnp.maximum(n, 1).astype(F32)
    large = max_exact + (jnp.log(nf / max_exact) / math.log(REL_MAX_DIST / max_exact)
                         * (REL_BUCKETS - max_exact)).astype(jnp.int32)
    large = jnp.minimum(large, REL_BUCKETS - 1)
    return jnp.where(n < max_exact, n, large)


def _near_bias_tiles(rel_bias, tq):
    assert tq >= REL_MAX_DIST
    n_heads = rel_bias.shape[1]
    dist = jnp.arange(2 * tq)
    shifted = (rel_bias[_rel_bucket(dist)] - rel_bias[REL_BUCKETS - 1]) * LOG2E
    by_offset = jnp.concatenate([jnp.full((tq - 1, n_heads), NEG, F32), shifted,
                                 jnp.zeros((1, n_heads), F32)], axis=0).T
    m = by_offset.shape[1]
    skew = jnp.tile(by_offset, (1, tq))[:, :tq * (m - 1)].reshape(n_heads, tq, m - 1)
    diag = skew[:, :, tq - 1:2 * tq - 1]
    prev = skew[:, :, 2 * tq - 1:3 * tq - 1]
    return jnp.stack([diag, prev, jnp.zeros_like(diag)], axis=1).astype(F32)


def _rotary_tables(seq):
    half = RET_QK_DIM // 2
    inv = ROPE_THETA ** (-jnp.arange(half, dtype=F32) / half)
    ang = jnp.arange(seq).astype(F32)[:, None] * inv[None, :]
    cos, sin = jnp.cos(ang), jnp.sin(ang)
    zero = jnp.zeros_like(sin)
    reps = LANES // RET_QK_DIM
    cos_t = jnp.tile(jnp.concatenate([cos, cos], axis=1), (1, reps))
    sina_t = jnp.tile(jnp.concatenate([-sin, zero], axis=1), (1, reps))
    sinb_t = jnp.tile(jnp.concatenate([zero, sin], axis=1), (1, reps))
    return cos_t, sina_t, sinb_t


@jax.jit
def _forward(x, rel_bias, norm_mix_g, w_in, b_gate, da_q_norm_g, da_k_norm_g, da_lambda,
             da_subln_g, ret_norm_g, sc_conv_w, sc_conv_b, w_branch_da, w_branch_ret,
             w_branch_sc, w_out, norm_ffn_g, w_ffn_in, ffn_conv_w, ffn_conv_b, w_ffn_out):
    b, s, d = x.shape
    depth = w_in.shape[0]
    assert d == D_MODEL and s % ROW_BLOCK == 0 and s % ATT_BLOCK == 0
    nb = _near_bias_tiles(rel_bias.astype(F32), ATT_BLOCK)
    cos_t, sina_t, sinb_t = _rotary_tables(s)
    group = jnp.arange(256) // 64
    bd = (group[:, None] == group[None, :]).astype(BF16)

    x2d = x.reshape(b * s, d)
    for l in range(depth):
        lam_init = 0.8 - 0.6 * math.exp(-0.3 * l)
        q, k, vt, oret, osc, gates = _in_proj(
            x2d, s, norm_mix_g[l][None, :], w_in[l].astype(BF16), b_gate[l][None, :],
            jnp.tile(da_q_norm_g[l], 256 // DA_QK_DIM)[None, :],
            jnp.tile(da_k_norm_g[l], 256 // DA_QK_DIM)[None, :],
            cos_t, sina_t, sinb_t, sc_conv_w[l], sc_conv_b[l][None, :], bd,
            jnp.tile(ret_norm_g[l], RET_HEADS)[None, :])
        to3 = lambda a: a.reshape(b, s, a.shape[-1])
        oda = _diff_attn(to3(q), to3(k), vt, nb, da_lambda[l].astype(F32),
                         da_subln_g[l][:, None], lam_init)
        x2d = _merge_ffn(
            x2d, s, oda.reshape(b * s, DA_V_W), oret, osc, gates,
            w_branch_da[l].astype(BF16), w_branch_ret[l].astype(BF16), w_branch_sc[l].astype(BF16),
            w_out[l].astype(BF16), norm_ffn_g[l][None, :],
            w_ffn_in[l].astype(BF16), ffn_conv_w[l], ffn_conv_b[l][None, :],
            w_ffn_out[l].astype(BF16))
    return x2d.reshape(b, s, d)


def kernel(x, rel_bias, norm_mix_g, w_in, b_gate, da_q_norm_g, da_k_norm_g, da_lambda, da_subln_g,
           ret_norm_g, sc_conv_w, sc_conv_b, w_branch_da, w_branch_ret, w_branch_sc, w_out,
           norm_ffn_g, w_ffn_in, ffn_conv_w, ffn_conv_b, w_ffn_out):
    return _forward(x, rel_bias, norm_mix_g, w_in, b_gate, da_q_norm_g, da_k_norm_g, da_lambda,
                    da_subln_g, ret_norm_g, sc_conv_w, sc_conv_b, w_branch_da, w_branch_ret,
                    w_branch_sc, w_out, norm_ffn_g, w_ffn_in, ffn_conv_w, ffn_conv_b, w_ffn_out)
```

```python
import functools
import math

import jax
import jax.numpy as jnp
import numpy as np
from jax import lax
from jax.experimental import pallas as pl
from jax.experimental.pallas import tpu as pltpu

D_MODEL = 1024
DA_HEADS = 4
DA_QK_DIM = 64
DA_V_DIM = 2 * DA_QK_DIM
RET_HEADS = 4
RET_QK_DIM = 64
RET_V_DIM = 64
SC_WIDTH = 256
CONV_W = 3
D_FF = 2816
REL_BUCKETS = 32
REL_MAX_DIST = 128
ROPE_THETA = 10000.0
EPS = 1e-6
N_BRANCH = 3

DA_QK_W = DA_HEADS * 2 * DA_QK_DIM
DA_V_W = DA_HEADS * DA_V_DIM
RET_QK_W = RET_HEADS * RET_QK_DIM
RET_V_W = RET_HEADS * RET_V_DIM

OFF_DA_Q = 0
OFF_DA_K = OFF_DA_Q + DA_QK_W
OFF_DA_V = OFF_DA_K + DA_QK_W
OFF_R_Q = OFF_DA_V + DA_V_W
OFF_R_K = OFF_R_Q + RET_QK_W
OFF_R_V = OFF_R_K + RET_QK_W
OFF_R_G = OFF_R_V + RET_V_W
OFF_SC_B = OFF_R_G + RET_V_W
OFF_SC_C = OFF_SC_B + SC_WIDTH
OFF_SC_X = OFF_SC_C + SC_WIDTH
OFF_GATE = OFF_SC_X + SC_WIDTH
IN_WIDTH = OFF_GATE + N_BRANCH * D_MODEL

LANES = 128
SUBLANES = 8
ROW_BLOCK = 512
ATT_BLOCK = 512
RET_CHUNK = 128
GATE_CHUNK = 512
FF_CHUNK = 256
N_FF_CHUNKS = D_FF // FF_CHUNK
VMEM_LIMIT = 56 * 1024 * 1024
NEG = -1e30
LOG2E = math.log2(math.e)
BF16 = jnp.bfloat16
F32 = jnp.float32


def _resident(shape):
    nd = len(shape)
    return pl.BlockSpec(shape, lambda *_: (0,) * nd, pipeline_mode=pl.Buffered(1))


def _group_mean_sq(y, bd):
    return jnp.dot((y * y).astype(BF16), bd, preferred_element_type=F32) * (1.0 / 64.0)


def _sigmoid(z):
    return 1.0 / (1.0 + jnp.exp(-z))


def _in_proj_kernel(x_ref, g_ref, w_ref, bg_ref, qg_ref, kg_ref, cos_ref, sina_ref, sinb_ref,
                    cw_ref, cb_ref, bd_ref, rn_ref,
                    q_out, k_out, vt_out, ret_out, sc_out, gate_out,
                    h_ref, ext_ref, state_ref, *, blocks_per_seq):
    tm = x_ref.shape[0]

    @pl.when(pl.program_id(0) % blocks_per_seq == 0)
    def _():
        ext_ref[0:SUBLANES, :] = jnp.zeros((SUBLANES, SC_WIDTH), F32)
        state_ref[...] = jnp.zeros_like(state_ref)

    x = x_ref[...]
    ms = jnp.mean(x * x, axis=-1, keepdims=True)
    h_ref[...] = (x * lax.rsqrt(ms + EPS) * g_ref[...]).astype(BF16)
    bd = bd_ref[...]

    def proj(off, width):
        return jnp.dot(h_ref[...], w_ref[:, off:off + width], preferred_element_type=F32)

    halves = range(DA_QK_W // 256)
    yq = [proj(OFF_DA_Q + half * 256, 256) for half in halves]
    yk = [proj(OFF_DA_K + half * 256, 256) for half in halves]
    v = proj(OFF_DA_V, DA_V_W)
    for c in range(tm // ATT_BLOCK):
        vt_out[0, c] = v[c * ATT_BLOCK:(c + 1) * ATT_BLOCK, :].T.astype(BF16)

    def rotary(y):
        outs = []
        for c in range(RET_QK_W // LANES):
            yc = y[:, c * LANES:(c + 1) * LANES]
            outs.append(yc * cos_ref[...]
                        + pltpu.roll(yc, LANES - 32, axis=1) * sina_ref[...]
                        + pltpu.roll(yc, 32, axis=1) * sinb_ref[...])
        return jnp.concatenate(outs, axis=1)

    rq = rotary(proj(OFF_R_Q, RET_QK_W)).astype(BF16)
    rk = (rotary(proj(OFF_R_K, RET_QK_W)) * (RET_QK_DIM ** -0.5)).astype(BF16)

    for half in halves:
        sl = slice(half * 256, (half + 1) * 256)
        q_out[:, sl] = (yq[half] * lax.rsqrt(_group_mean_sq(yq[half], bd) + EPS)
                        * (qg_ref[...] * (DA_QK_DIM ** -0.5 * LOG2E))).astype(BF16)
        k_out[:, sl] = (yk[half] * lax.rsqrt(_group_mean_sq(yk[half], bd) + EPS)
                        * kg_ref[...]).astype(BF16)

    rv = proj(OFF_R_V, RET_V_W).astype(BF16)
    ret_steps = _retention_steps(rq, rk, rv, proj(OFF_R_G, RET_V_W), rn_ref[...], bd, state_ref,
                                 ret_out)

    u = proj(OFF_SC_C, SC_WIDTH) * proj(OFF_SC_X, SC_WIDTH)
    ext_ref[SUBLANES:SUBLANES + tm, :] = u
    conv = (cb_ref[...]
            + cw_ref[0:1, :] * ext_ref[SUBLANES - 2:SUBLANES - 2 + tm, :]
            + cw_ref[1:2, :] * ext_ref[SUBLANES - 1:SUBLANES - 1 + tm, :]
            + cw_ref[2:3, :] * u)
    ext_ref[0:SUBLANES, :] = u[tm - SUBLANES:, :]
    sc_out[...] = (proj(OFF_SC_B, SC_WIDTH) * conv).astype(BF16)

    for c in range(N_BRANCH * D_MODEL // GATE_CHUNK):
        next(ret_steps, None)
        sl = slice(c * GATE_CHUNK, (c + 1) * GATE_CHUNK)
        z = proj(OFF_GATE + c * GATE_CHUNK, GATE_CHUNK) + bg_ref[:, sl]
        gate_out[:, sl] = _sigmoid(z).astype(BF16)
    for _ in ret_steps:
        pass


def _in_proj(x2d, seq, g, w, bg, qg, kg, cos_t, sina_t, sinb_t, cw, cb, bd, rn):
    t = x2d.shape[0]
    tm = ROW_BLOCK
    blocks_per_seq = seq // tm
    row = lambda width: pl.BlockSpec((tm, width), lambda i: (i, 0))
    tab = pl.BlockSpec((tm, LANES), lambda i: (i % blocks_per_seq, 0))
    out_widths = (DA_QK_W, DA_QK_W, RET_V_W, SC_WIDTH, N_BRANCH * D_MODEL)
    out_specs = [row(wd) for wd in out_widths]
    out_shape = [jax.ShapeDtypeStruct((t, wd), BF16) for wd in out_widths]
    out_specs.insert(2, pl.BlockSpec((1, tm // ATT_BLOCK, DA_V_W, ATT_BLOCK),
                                     lambda i: (i // blocks_per_seq, i % blocks_per_seq, 0, 0)))
    out_shape.insert(2, jax.ShapeDtypeStruct((t // seq, seq // ATT_BLOCK, DA_V_W, ATT_BLOCK), BF16))
    return pl.pallas_call(
        functools.partial(_in_proj_kernel, blocks_per_seq=blocks_per_seq),
        grid=(t // tm,),
        in_specs=[row(D_MODEL), _resident(g.shape), _resident(w.shape), _resident(bg.shape),
                  _resident(qg.shape), _resident(kg.shape), tab, tab, tab,
                  _resident(cw.shape), _resident(cb.shape), _resident(bd.shape),
                  _resident(rn.shape)],
        out_specs=out_specs,
        out_shape=out_shape,
        scratch_shapes=[pltpu.VMEM((tm, D_MODEL), BF16),
                        pltpu.VMEM((tm + SUBLANES, SC_WIDTH), F32),
                        pltpu.VMEM((RET_QK_W, RET_V_W), F32)],
        compiler_params=pltpu.CompilerParams(dimension_semantics=("arbitrary",),
                                             vmem_limit_bytes=VMEM_LIMIT),
        name="in_proj",
    )(x2d, g, w, bg, qg, kg, cos_t, sina_t, sinb_t, cw, cb, bd, rn)


def _diff_attn_kernel(q_ref, k_ref, vt_ref, nb_ref, lam_ref, sg_ref, o_ref,
                      s0_ref, s1_ref, acc0_ref, acc1_ref, *, lam_init):
    tq = o_ref.shape[1]
    qi = pl.program_id(2)
    n_q = pl.num_programs(2)

    def masked_q(block):
        q = q_ref[0, pl.ds(pl.multiple_of(block * tq, tq), tq), :]
        lane = lax.broadcasted_iota(jnp.int32, q.shape, 1)
        zero = jnp.zeros_like(q)
        return jnp.where(lane < DA_QK_DIM, q, zero), jnp.where(lane >= DA_QK_DIM, q, zero)

    q_map = masked_q(qi)
    s_refs = (s0_ref, s1_ref)
    acc_refs = (acc0_ref, acc1_ref)

    def put_scores(mp, j, far=False, q_block=qi, q_rows=None):
        start = pl.multiple_of(j * tq, tq)
        q_rows = q_map[mp] if q_rows is None else q_rows
        s = lax.dot_general(k_ref[0, pl.ds(start, tq), :], q_rows, (((1,), (1,)), ((), ())),
                            preferred_element_type=F32)
        s_refs[mp][...] = s if far else s + nb_ref[0, jnp.minimum(q_block - j, 2)]

    def update(mp, j, stats):
        m, l = stats
        m_new = jnp.maximum(m, jnp.max(s_refs[mp][...], axis=0, keepdims=True))
        alpha = jnp.exp2(m - m_new)
        p = jnp.exp2(s_refs[mp][...] - m_new)
        l = alpha * l + jnp.sum(p, axis=0, keepdims=True)
        acc_refs[mp][...] = alpha * acc_refs[mp][...] + jnp.dot(
            vt_ref[0, j], p.astype(BF16), preferred_element_type=F32)
        return m_new, l

    def step(j, stats, far=False):
        put_scores(1, j, far)
        stats0 = update(0, j, stats[0])
        put_scores(0, j + 1, far)
        stats1 = update(1, j, stats[1])
        return stats0, stats1

    init = (jnp.full((1, tq), NEG, F32), jnp.zeros((1, tq), F32))
    acc0_ref[...] = jnp.zeros_like(acc0_ref)
    acc1_ref[...] = jnp.zeros_like(acc1_ref)

    @pl.when(qi == 0)
    def _():
        put_scores(0, 0)

    n_far = jnp.maximum(qi - 2, 0)
    stats = lax.fori_loop(0, n_far, functools.partial(step, far=True), (init, init))
    stats = lax.fori_loop(n_far, qi, step, stats)
    put_scores(1, qi)
    _, l0 = update(0, qi, stats[0])
    nxt = jnp.minimum(qi + 1, n_q - 1)
    put_scores(0, 0, q_block=nxt, q_rows=masked_q(nxt)[0])
    _, l1 = update(1, qi, stats[1])

    lp = lam_ref[...]
    lam = (jnp.exp(jnp.sum(lp[0:1] * lp[1:2], axis=1, keepdims=True))
           - jnp.exp(jnp.sum(lp[2:3] * lp[3:4], axis=1, keepdims=True)) + lam_init)
    o = acc0_ref[...] * (1.0 / l0) - lam * (acc1_ref[...] * (1.0 / l1))
    ms = jnp.mean(o * o, axis=0, keepdims=True)
    o_ref[0] = (o * lax.rsqrt(ms + EPS) * (sg_ref[...] * (1.0 - lam_init))).T.astype(BF16)


def _diff_attn(q, k, vt, nb, lam_p, sg, lam_init):
    b, s, _ = q.shape
    tq = ATT_BLOCK
    return pl.pallas_call(
        functools.partial(_diff_attn_kernel, lam_init=lam_init),
        grid=(b, DA_HEADS, s // tq),
        in_specs=[pl.BlockSpec((1, s, LANES), lambda bi, h, i: (bi, 0, h)),
                  pl.BlockSpec((1, s, LANES), lambda bi, h, i: (bi, 0, h)),
                  pl.BlockSpec((1, s // tq, DA_V_DIM, tq), lambda bi, h, i: (bi, 0, h, 0)),
                  pl.BlockSpec((1, 3, tq, tq), lambda bi, h, i: (h, 0, 0, 0)),
                  pl.BlockSpec(lam_p.shape, lambda bi, h, i: (0, 0)),
                  pl.BlockSpec(sg.shape, lambda bi, h, i: (0, 0))],
        out_specs=pl.BlockSpec((1, tq, LANES), lambda bi, h, i: (bi, i, h)),
        out_shape=jax.ShapeDtypeStruct((b, s, DA_V_W), BF16),
        scratch_shapes=[pltpu.VMEM((tq, tq), F32), pltpu.VMEM((tq, tq), F32),
                        pltpu.VMEM((DA_V_DIM, tq), F32), pltpu.VMEM((DA_V_DIM, tq), F32)],
        compiler_params=pltpu.CompilerParams(
            dimension_semantics=("arbitrary", "arbitrary", "arbitrary"),
            vmem_limit_bytes=VMEM_LIMIT),
        name="diff_attn",
    )(q, k, vt, nb, lam_p, sg)


def _ret_log_gamma(head):
    lg = [math.log(1.0 - 2.0 ** (-5.0 - h)) for h in range(RET_HEADS)]
    out = jnp.full(head.shape, lg[RET_HEADS - 1], F32)
    for h in range(RET_HEADS - 2, -1, -1):
        out = jnp.where(head == h, lg[h], out)
    return out


def _retention_steps(q, k, v, gate, ng, bd, state_ref, o_ref):
    c = RET_CHUNK
    w = RET_QK_W
    n_chunks = q.shape[0] // c
    head_shift = RET_QK_DIM.bit_length() - 1
    chunk_shift = c.bit_length() - 1
    assert 1 << head_shift == RET_QK_DIM == RET_V_DIM and 1 << chunk_shift == c
    lane_head = lax.broadcasted_iota(jnp.int32, (c, w), 1) >> head_shift
    row = lax.broadcasted_iota(jnp.int32, (c, w), 0).astype(F32)
    lg_lane = _ret_log_gamma(lane_head)
    q_decay = jnp.exp(lg_lane * (row + 1.0))
    k_decay = jnp.exp(lg_lane * (c - 1.0 - row))
    chunk_decay = jnp.exp(lg_lane[0:1, :] * float(c))
    srow = lax.broadcasted_iota(jnp.int32, (RET_HEADS * c, c), 0)
    scol = lax.broadcasted_iota(jnp.int32, (RET_HEADS * c, c), 1)
    diff = ((srow & (c - 1)) - scol).astype(F32)
    d_intra = jnp.where(diff >= 0,
                        jnp.exp(_ret_log_gamma(srow >> chunk_shift) * jnp.maximum(diff, 0.0)), 0.0)
    st_row_head = lax.broadcasted_iota(jnp.int32, (w, w), 0) >> head_shift
    st_col_head = lax.broadcasted_iota(jnp.int32, (w, w), 1) >> head_shift
    same_head = st_row_head == st_col_head

    for n in range(n_chunks):
        sl = slice(n * c, (n + 1) * c)
        qc, kc, vc = q[sl], k[sl], v[sl]
        zero = jnp.zeros_like(qc)
        qs = jnp.concatenate([jnp.where(lane_head == h, qc, zero) for h in range(RET_HEADS)], axis=0)
        s = lax.dot_general(qs, kc, (((1,), (1,)), ((), ())), preferred_element_type=F32)
        state = state_ref[...]
        cross = jnp.dot(qc, state.astype(BF16), preferred_element_type=F32) * q_decay
        k_dec = (kc.astype(F32) * k_decay).T.astype(BF16)
        kv = jnp.dot(k_dec, vc, preferred_element_type=F32)
        state_ref[...] = state * chunk_decay + jnp.where(same_head, kv, 0.0)
        yield
        o_all = jnp.dot((s * d_intra).astype(BF16), vc, preferred_element_type=F32)
        inner = jnp.zeros((c, w), F32)
        for h in range(RET_HEADS):
            inner = jnp.where(lane_head == h, o_all[h * c:(h + 1) * c], inner)
        o = inner + cross

        y = o * lax.rsqrt(_group_mean_sq(o, bd) + EPS) * ng
        g = gate[sl]
        o_ref[sl, :] = (y * (g * _sigmoid(g))).astype(BF16)


def _merge_ffn_kernel(x_ref, oda_ref, oret_ref, osc_ref, gate_ref, wda_ref, wret_ref, wsc_ref,
                      wout_ref, ng_ref, win_ref, cw_ref, cb_ref, wo_ref,
                      out_ref, h_ref, ext_ref, carry_ref, act_ref, *, blocks_per_seq):
    tm = x_ref.shape[0]
    d = D_MODEL
    y = (gate_ref[:, 0:d].astype(F32)
         * jnp.dot(oda_ref[...], wda_ref[...], preferred_element_type=F32))
    y += (gate_ref[:, d:2 * d].astype(F32)
          * jnp.dot(oret_ref[...], wret_ref[...], preferred_element_type=F32))
    y += (gate_ref[:, 2 * d:3 * d].astype(F32)
          * jnp.dot(osc_ref[...], wsc_ref[...], preferred_element_type=F32))
    x1 = x_ref[...] + jnp.dot(y.astype(BF16), wout_ref[...], preferred_element_type=F32)
    out_ref[...] = x1
    ms = jnp.mean(x1 * x1, axis=-1, keepdims=True)
    h_ref[...] = (x1 * lax.rsqrt(ms + EPS) * ng_ref[...]).astype(BF16)

    @pl.when(pl.program_id(0) % blocks_per_seq == 0)
    def _():
        carry_ref[...] = jnp.zeros_like(carry_ref)

    for c in range(N_FF_CHUNKS):
        sl = slice(c * FF_CHUNK, (c + 1) * FF_CHUNK)
        g = jnp.dot(h_ref[...], win_ref[:, sl], preferred_element_type=F32)
        u = jnp.dot(h_ref[...], win_ref[:, D_FF + c * FF_CHUNK:D_FF + (c + 1) * FF_CHUNK],
                    preferred_element_type=F32)
        ext_ref[0:SUBLANES, :] = carry_ref[:, sl]
        ext_ref[SUBLANES:SUBLANES + tm, :] = g
        gc = (cb_ref[:, sl]
              + cw_ref[0:1, sl] * ext_ref[SUBLANES - 2:SUBLANES - 2 + tm, :]
              + cw_ref[1:2, sl] * ext_ref[SUBLANES - 1:SUBLANES - 1 + tm, :]
              + cw_ref[2:3, sl] * g)
        carry_ref[:, sl] = g[tm - SUBLANES:, :]
        act_ref[:, sl] = (gc * _sigmoid(gc) * u).astype(BF16)

    out_ref[...] += jnp.dot(act_ref[...], wo_ref[...], preferred_element_type=F32)


def _merge_ffn(x2d, seq, oda, oret, osc, gates, wda, wret, wsc, wout, ng, win, cw, cb, wo):
    t = x2d.shape[0]
    tm = ROW_BLOCK
    blocks_per_seq = seq // tm
    row = lambda width: pl.BlockSpec((tm, width), lambda i: (i, 0))
    return pl.pallas_call(
        functools.partial(_merge_ffn_kernel, blocks_per_seq=blocks_per_seq),
        grid=(t // tm,),
        in_specs=[row(D_MODEL), row(DA_V_W), row(RET_V_W), row(SC_WIDTH), row(N_BRANCH * D_MODEL)]
                 + [_resident(a.shape) for a in (wda, wret, wsc, wout, ng, win, cw, cb, wo)],
        out_specs=row(D_MODEL),
        out_shape=jax.ShapeDtypeStruct((t, D_MODEL), F32),
        scratch_shapes=[pltpu.VMEM((tm, D_MODEL), BF16),
                        pltpu.VMEM((tm + SUBLANES, FF_CHUNK), F32),
                        pltpu.VMEM((SUBLANES, D_FF), F32),
                        pltpu.VMEM((tm, D_FF), BF16)],
        compiler_params=pltpu.CompilerParams(dimension_semantics=("arbitrary",),
                                             vmem_limit_bytes=VMEM_LIMIT),
        name="merge_ffn",
    )(x2d, oda, oret, osc, gates, wda, wret, wsc, wout, ng, win, cw, cb, wo)


def _rel_bucket(n):
    max_exact = REL_BUCKETS // 2
    nf = jnp.maximum(n, 1).astype(F32)
    large = max_exact + (jnp.log(nf / max_exact) / math.log(REL_MAX_DIST / max_exact)
                         * (REL_BUCKETS - max_exact)).astype(jnp.int32)
    large = jnp.minimum(large, REL_BUCKETS - 1)
    return jnp.where(n < max_exact, n, large)


def _near_bias_tiles(rel_bias, tq):
    assert tq >= REL_MAX_DIST
    n_heads = rel_bias.shape[1]
    dist = jnp.arange(2 * tq)
    shifted = (rel_bias[_rel_bucket(dist)] - rel_bias[REL_BUCKETS - 1]) * LOG2E
    by_offset = jnp.concatenate([jnp.full((tq - 1, n_heads), NEG, F32), shifted,
                                 jnp.zeros((1, n_heads), F32)], axis=0).T
    m = by_offset.shape[1]
    skew = jnp.tile(by_offset, (1, tq))[:, :tq * (m - 1)].reshape(n_heads, tq, m - 1)
    diag = skew[:, :, tq - 1:2 * tq - 1]
    prev = skew[:, :, 2 * tq - 1:3 * tq - 1]
    return jnp.stack([diag, prev, jnp.zeros_like(diag)], axis=1).astype(F32)


def _rotary_tables(seq):
    half = RET_QK_DIM // 2
    inv = ROPE_THETA ** (-jnp.arange(half, dtype=F32) / half)
    ang = jnp.arange(seq).astype(F32)[:, None] * inv[None, :]
    cos, sin = jnp.cos(ang), jnp.sin(ang)
    zero = jnp.zeros_like(sin)
    reps = LANES // RET_QK_DIM
    cos_t = jnp.tile(jnp.concatenate([cos, cos], axis=1), (1, reps))
    sina_t = jnp.tile(jnp.concatenate([-sin, zero], axis=1), (1, reps))
    sinb_t = jnp.tile(jnp.concatenate([zero, sin], axis=1), (1, reps))
    return cos_t, sina_t, sinb_t


@jax.jit
def _forward(x, rel_bias, norm_mix_g, w_in, b_gate, da_q_norm_g, da_k_norm_g, da_lambda,
             da_subln_g, ret_norm_g, sc_conv_w, sc_conv_b, w_branch_da, w_branch_ret,
             w_branch_sc, w_out, norm_ffn_g, w_ffn_in, ffn_conv_w, ffn_conv_b, w_ffn_out):
    b, s, d = x.shape
    depth = w_in.shape[0]
    assert d == D_MODEL and s % ROW_BLOCK == 0 and s % ATT_BLOCK == 0
    nb = _near_bias_tiles(rel_bias.astype(F32), ATT_BLOCK)
    cos_t, sina_t, sinb_t = _rotary_tables(s)
    group = jnp.arange(256) // 64
    bd = (group[:, None] == group[None, :]).astype(BF16)

    x2d = x.reshape(b * s, d)
    for l in range(depth):
        lam_init = 0.8 - 0.6 * math.exp(-0.3 * l)
        q, k, vt, oret, osc, gates = _in_proj(
            x2d, s, norm_mix_g[l][None, :], w_in[l].astype(BF16), b_gate[l][None, :],
            jnp.tile(da_q_norm_g[l], 256 // DA_QK_DIM)[None, :],
            jnp.tile(da_k_norm_g[l], 256 // DA_QK_DIM)[None, :],
            cos_t, sina_t, sinb_t, sc_conv_w[l], sc_conv_b[l][None, :], bd,
            jnp.tile(ret_norm_g[l], RET_HEADS)[None, :])
        to3 = lambda a: a.reshape(b, s, a.shape[-1])
        oda = _diff_attn(to3(q), to3(k), vt, nb, da_lambda[l].astype(F32),
                         da_subln_g[l][:, None], lam_init)
        x2d = _merge_ffn(
            x2d, s, oda.reshape(b * s, DA_V_W), oret, osc, gates,
            w_branch_da[l].astype(BF16), w_branch_ret[l].astype(BF16), w_branch_sc[l].astype(BF16),
            w_out[l].astype(BF16), norm_ffn_g[l][None, :],
            w_ffn_in[l].astype(BF16), ffn_conv_w[l], ffn_conv_b[l][None, :],
            w_ffn_out[l].astype(BF16))
    return x2d.reshape(b, s, d)


def kernel(x, rel_bias, norm_mix_g, w_in, b_gate, da_q_norm_g, da_k_norm_g, da_lambda, da_subln_g,
           ret_norm_g, sc_conv_w, sc_conv_b, w_branch_da, w_branch_ret, w_branch_sc, w_out,
           norm_ffn_g, w_ffn_in, ffn_conv_w, ffn_conv_b, w_ffn_out):
    return _forward(x, rel_bias, norm_mix_g, w_in, b_gate, da_q_norm_g, da_k_norm_g, da_lambda,
                    da_subln_g, ret_norm_g, sc_conv_w, sc_conv_b, w_branch_da, w_branch_ret,
                    w_branch_sc, w_out, norm_ffn_g, w_ffn_in, ffn_conv_w, ffn_conv_b, w_ffn_out)
```

```python
import functools
import math

import jax
import jax.numpy as jnp
import numpy as np
from jax import lax
from jax.experimental import pallas as pl
from jax.experimental.pallas import tpu as pltpu

D_MODEL = 1024
DA_HEADS = 4
DA_QK_DIM = 64
DA_V_DIM = 2 * DA_QK_DIM
RET_HEADS = 4
RET_QK_DIM = 64
RET_V_DIM = 64
SC_WIDTH = 256
CONV_W = 3
D_FF = 2816
REL_BUCKETS = 32
REL_MAX_DIST = 128
ROPE_THETA = 10000.0
EPS = 1e-6
N_BRANCH = 3

DA_QK_W = DA_HEADS * 2 * DA_QK_DIM
DA_V_W = DA_HEADS * DA_V_DIM
RET_QK_W = RET_HEADS * RET_QK_DIM
RET_V_W = RET_HEADS * RET_V_DIM

OFF_DA_Q = 0
OFF_DA_K = OFF_DA_Q + DA_QK_W
OFF_DA_V = OFF_DA_K + DA_QK_W
OFF_R_Q = OFF_DA_V + DA_V_W
OFF_R_K = OFF_R_Q + RET_QK_W
OFF_R_V = OFF_R_K + RET_QK_W
OFF_R_G = OFF_R_V + RET_V_W
OFF_SC_B = OFF_R_G + RET_V_W
OFF_SC_C = OFF_SC_B + SC_WIDTH
OFF_SC_X = OFF_SC_C + SC_WIDTH
OFF_GATE = OFF_SC_X + SC_WIDTH
IN_WIDTH = OFF_GATE + N_BRANCH * D_MODEL

LANES = 128
SUBLANES = 8
ROW_BLOCK = 512
ATT_BLOCK = 512
RET_CHUNK = 128
GATE_CHUNK = 512
FF_CHUNK = 256
N_FF_CHUNKS = D_FF // FF_CHUNK
VMEM_LIMIT = 56 * 1024 * 1024
NEG = -1e30
LOG2E = math.log2(math.e)
BF16 = jnp.bfloat16
F32 = jnp.float32


def _resident(shape):
    nd = len(shape)
    return pl.BlockSpec(shape, lambda *_: (0,) * nd, pipeline_mode=pl.Buffered(1))


def _resident_layer(stacked_shape, layer):
    _, rows, cols = stacked_shape
    return pl.BlockSpec((None, rows, cols), lambda *_: (layer, 0, 0), pipeline_mode=pl.Buffered(1))


def _group_mean_sq(y, bd):
    return jnp.dot((y * y).astype(BF16), bd, preferred_element_type=F32) * (1.0 / 64.0)


def _sigmoid(z):
    return 1.0 / (1.0 + jnp.exp(-z))


def _in_proj_kernel(x_ref, g_ref, w_ref, bg_ref, qg_ref, kg_ref, cos_ref, sina_ref, sinb_ref,
                    cw_ref, cb_ref, bd_ref, rn_ref,
                    q_out, k_out, vt_out, ret_out, sc_out, gate_out,
                    h_ref, ext_ref, state_ref, *, blocks_per_seq):
    tm = x_ref.shape[0]

    @pl.when(pl.program_id(0) % blocks_per_seq == 0)
    def _():
        ext_ref[0:SUBLANES, :] = jnp.zeros((SUBLANES, SC_WIDTH), F32)
        state_ref[...] = jnp.zeros_like(state_ref)

    x = x_ref[...]
    ms = jnp.mean(x * x, axis=-1, keepdims=True)
    h_ref[...] = (x * lax.rsqrt(ms + EPS) * g_ref[...]).astype(BF16)
    bd = bd_ref[...]

    def proj(off, width):
        return jnp.dot(h_ref[...], w_ref[:, off:off + width], preferred_element_type=F32)

    halves = range(DA_QK_W // 256)
    yq = [proj(OFF_DA_Q + half * 256, 256) for half in halves]
    yk = [proj(OFF_DA_K + half * 256, 256) for half in halves]
    v = proj(OFF_DA_V, DA_V_W)
    for c in range(tm // ATT_BLOCK):
        vt_out[0, c] = v[c * ATT_BLOCK:(c + 1) * ATT_BLOCK, :].T.astype(BF16)

    def rotary(y):
        outs = []
        for c in range(RET_QK_W // LANES):
            yc = y[:, c * LANES:(c + 1) * LANES]
            outs.append(yc * cos_ref[...]
                        + pltpu.roll(yc, LANES - 32, axis=1) * sina_ref[...]
                        + pltpu.roll(yc, 32, axis=1) * sinb_ref[...])
        return jnp.concatenate(outs, axis=1)

    rq = rotary(proj(OFF_R_Q, RET_QK_W)).astype(BF16)
    rk = (rotary(proj(OFF_R_K, RET_QK_W)) * (RET_QK_DIM ** -0.5)).astype(BF16)

    for half in halves:
        sl = slice(half * 256, (half + 1) * 256)
        q_out[:, sl] = (yq[half] * lax.rsqrt(_group_mean_sq(yq[half], bd) + EPS)
                        * (qg_ref[...] * (DA_QK_DIM ** -0.5 * LOG2E))).astype(BF16)
        k_out[:, sl] = (yk[half] * lax.rsqrt(_group_mean_sq(yk[half], bd) + EPS)
                        * kg_ref[...]).astype(BF16)

    rv = proj(OFF_R_V, RET_V_W).astype(BF16)
    ret_steps = _retention_steps(rq, rk, rv, proj(OFF_R_G, RET_V_W), rn_ref[...], bd, state_ref,
                                 ret_out)

    u = proj(OFF_SC_C, SC_WIDTH) * proj(OFF_SC_X, SC_WIDTH)
    ext_ref[SUBLANES:SUBLANES + tm, :] = u
    conv = (cb_ref[...]
            + cw_ref[0:1, :] * ext_ref[SUBLANES - 2:SUBLANES - 2 + tm, :]
            + cw_ref[1:2, :] * ext_ref[SUBLANES - 1:SUBLANES - 1 + tm, :]
            + cw_ref[2:3, :] * u)
    ext_ref[0:SUBLANES, :] = u[tm - SUBLANES:, :]
    sc_out[...] = (proj(OFF_SC_B, SC_WIDTH) * conv).astype(BF16)

    for c in range(N_BRANCH * D_MODEL // GATE_CHUNK):
        next(ret_steps, None)
        sl = slice(c * GATE_CHUNK, (c + 1) * GATE_CHUNK)
        z = proj(OFF_GATE + c * GATE_CHUNK, GATE_CHUNK) + bg_ref[:, sl]
        gate_out[:, sl] = _sigmoid(z).astype(BF16)
    for _ in ret_steps:
        pass


def _in_proj(x2d, seq, layer, g, w, bg, qg, kg, cos_t, sina_t, sinb_t, cw, cb, bd, rn):
    t = x2d.shape[0]
    tm = ROW_BLOCK
    blocks_per_seq = seq // tm
    row = lambda width: pl.BlockSpec((tm, width), lambda i: (i, 0))
    tab = pl.BlockSpec((tm, LANES), lambda i: (i % blocks_per_seq, 0))
    out_widths = (DA_QK_W, DA_QK_W, RET_V_W, SC_WIDTH, N_BRANCH * D_MODEL)
    out_specs = [row(wd) for wd in out_widths]
    out_shape = [jax.ShapeDtypeStruct((t, wd), BF16) for wd in out_widths]
    out_specs.insert(2, pl.BlockSpec((1, tm // ATT_BLOCK, DA_V_W, ATT_BLOCK),
                                     lambda i: (i // blocks_per_seq, i % blocks_per_seq, 0, 0)))
    out_shape.insert(2, jax.ShapeDtypeStruct((t // seq, seq // ATT_BLOCK, DA_V_W, ATT_BLOCK), BF16))
    return pl.pallas_call(
        functools.partial(_in_proj_kernel, blocks_per_seq=blocks_per_seq),
        grid=(t // tm,),
        in_specs=[row(D_MODEL), _resident(g.shape), _resident_layer(w.shape, layer),
                  _resident(bg.shape),
                  _resident(qg.shape), _resident(kg.shape), tab, tab, tab,
                  _resident(cw.shape), _resident(cb.shape), _resident(bd.shape),
                  _resident(rn.shape)],
        out_specs=out_specs,
        out_shape=out_shape,
        scratch_shapes=[pltpu.VMEM((tm, D_MODEL), BF16),
                        pltpu.VMEM((tm + SUBLANES, SC_WIDTH), F32),
                        pltpu.VMEM((RET_QK_W, RET_V_W), F32)],
        compiler_params=pltpu.CompilerParams(dimension_semantics=("arbitrary",),
                                             vmem_limit_bytes=VMEM_LIMIT),
        name="in_proj",
    )(x2d, g, w, bg, qg, kg, cos_t, sina_t, sinb_t, cw, cb, bd, rn)


def _diff_attn_kernel(q_ref, k_ref, vt_ref, nb_ref, lam_ref, sg_ref, o_ref,
                      s0_ref, s1_ref, acc0_ref, acc1_ref, *, lam_init):
    tq = o_ref.shape[1]
    qi = pl.program_id(2)
    n_q = pl.num_programs(2)

    def masked_q(block):
        q = q_ref[0, pl.ds(pl.multiple_of(block * tq, tq), tq), :]
        lane = lax.broadcasted_iota(jnp.int32, q.shape, 1)
        zero = jnp.zeros_like(q)
        return jnp.where(lane < DA_QK_DIM, q, zero), jnp.where(lane >= DA_QK_DIM, q, zero)

    q_map = masked_q(qi)
    s_refs = (s0_ref, s1_ref)
    acc_refs = (acc0_ref, acc1_ref)

    def put_scores(mp, j, far=False, q_block=qi, q_rows=None):
        start = pl.multiple_of(j * tq, tq)
        q_rows = q_map[mp] if q_rows is None else q_rows
        s = lax.dot_general(k_ref[0, pl.ds(start, tq), :], q_rows, (((1,), (1,)), ((), ())),
                            preferred_element_type=F32)
        s_refs[mp][...] = s if far else s + nb_ref[0, jnp.minimum(q_block - j, 2)]

    def update(mp, j, stats):
        m, l = stats
        m_new = jnp.maximum(m, jnp.max(s_refs[mp][...], axis=0, keepdims=True))
        alpha = jnp.exp2(m - m_new)
        p = jnp.exp2(s_refs[mp][...] - m_new)
        l = alpha * l + jnp.sum(p, axis=0, keepdims=True)
        acc_refs[mp][...] = alpha * acc_refs[mp][...] + jnp.dot(
            vt_ref[0, j], p.astype(BF16), preferred_element_type=F32)
        return m_new, l

    def step(j, stats, far=False):
        put_scores(1, j, far)
        stats0 = update(0, j, stats[0])
        put_scores(0, j + 1, far)
        stats1 = update(1, j, stats[1])
        return stats0, stats1

    init = (jnp.full((1, tq), NEG, F32), jnp.zeros((1, tq), F32))
    acc0_ref[...] = jnp.zeros_like(acc0_ref)
    acc1_ref[...] = jnp.zeros_like(acc1_ref)

    @pl.when(qi == 0)
    def _():
        put_scores(0, 0)

    n_far = jnp.maximum(qi - 2, 0)
    stats = lax.fori_loop(0, n_far, functools.partial(step, far=True), (init, init))
    stats = lax.fori_loop(n_far, qi, step, stats)
    put_scores(1, qi)
    _, l0 = update(0, qi, stats[0])
    nxt = jnp.minimum(qi + 1, n_q - 1)
    put_scores(0, 0, q_block=nxt, q_rows=masked_q(nxt)[0])
    _, l1 = update(1, qi, stats[1])

    lp = lam_ref[...]
    lam = (jnp.exp(jnp.sum(lp[0:1] * lp[1:2], axis=1, keepdims=True))
           - jnp.exp(jnp.sum(lp[2:3] * lp[3:4], axis=1, keepdims=True)) + lam_init)
    o = acc0_ref[...] * (1.0 / l0) - lam * (acc1_ref[...] * (1.0 / l1))
    ms = jnp.mean(o * o, axis=0, keepdims=True)
    o_ref[0] = (o * lax.rsqrt(ms + EPS) * (sg_ref[...] * (1.0 - lam_init))).T.astype(BF16)


def _diff_attn(q, k, vt, nb, lam_p, sg, lam_init):
    b, s, _ = q.shape
    tq = ATT_BLOCK
    return pl.pallas_call(
        functools.partial(_diff_attn_kernel, lam_init=lam_init),
        grid=(b, DA_HEADS, s // tq),
        in_specs=[pl.BlockSpec((1, s, LANES), lambda bi, h, i: (bi, 0, h)),
                  pl.BlockSpec((1, s, LANES), lambda bi, h, i: (bi, 0, h)),
                  pl.BlockSpec((1, s // tq, DA_V_DIM, tq), lambda bi, h, i: (bi, 0, h, 0)),
                  pl.BlockSpec((1, 3, tq, tq), lambda bi, h, i: (h, 0, 0, 0)),
                  pl.BlockSpec(lam_p.shape, lambda bi, h, i: (0, 0)),
                  pl.BlockSpec(sg.shape, lambda bi, h, i: (0, 0))],
        out_specs=pl.BlockSpec((1, tq, LANES), lambda bi, h, i: (bi, i, h)),
        out_shape=jax.ShapeDtypeStruct((b, s, DA_V_W), BF16),
        scratch_shapes=[pltpu.VMEM((tq, tq), F32), pltpu.VMEM((tq, tq), F32),
                        pltpu.VMEM((DA_V_DIM, tq), F32), pltpu.VMEM((DA_V_DIM, tq), F32)],
        compiler_params=pltpu.CompilerParams(
            dimension_semantics=("arbitrary", "arbitrary", "arbitrary"),
            vmem_limit_bytes=VMEM_LIMIT),
        name="diff_attn",
    )(q, k, vt, nb, lam_p, sg)


def _ret_log_gamma(head):
    lg = [math.log(1.0 - 2.0 ** (-5.0 - h)) for h in range(RET_HEADS)]
    out = jnp.full(head.shape, lg[RET_HEADS - 1], F32)
    for h in range(RET_HEADS - 2, -1, -1):
        out = jnp.where(head == h, lg[h], out)
    return out


def _retention_steps(q, k, v, gate, ng, bd, state_ref, o_ref):
    c = RET_CHUNK
    w = RET_QK_W
    n_chunks = q.shape[0] // c
    head_shift = RET_QK_DIM.bit_length() - 1
    chunk_shift = c.bit_length() - 1
    assert 1 << head_shift == RET_QK_DIM == RET_V_DIM and 1 << chunk_shift == c
    lane_head = lax.broadcasted_iota(jnp.int32, (c, w), 1) >> head_shift
    row = lax.broadcasted_iota(jnp.int32, (c, w), 0).astype(F32)
    lg_lane = _ret_log_gamma(lane_head)
    q_decay = jnp.exp(lg_lane * (row + 1.0))
    k_decay = jnp.exp(lg_lane * (c - 1.0 - row))
    chunk_decay = jnp.exp(lg_lane[0:1, :] * float(c))
    srow = lax.broadcasted_iota(jnp.int32, (RET_HEADS * c, c), 0)
    scol = lax.broadcasted_iota(jnp.int32, (RET_HEADS * c, c), 1)
    diff = ((srow & (c - 1)) - scol).astype(F32)
    d_intra = jnp.where(diff >= 0,
                        jnp.exp(_ret_log_gamma(srow >> chunk_shift) * jnp.maximum(diff, 0.0)), 0.0)
    st_row_head = lax.broadcasted_iota(jnp.int32, (w, w), 0) >> head_shift
    st_col_head = lax.broadcasted_iota(jnp.int32, (w, w), 1) >> head_shift
    same_head = st_row_head == st_col_head

    for n in range(n_chunks):
        sl = slice(n * c, (n + 1) * c)
        qc, kc, vc = q[sl], k[sl], v[sl]
        zero = jnp.zeros_like(qc)
        qs = jnp.concatenate([jnp.where(lane_head == h, qc, zero) for h in range(RET_HEADS)], axis=0)
        s = lax.dot_general(qs, kc, (((1,), (1,)), ((), ())), preferred_element_type=F32)
        state = state_ref[...]
        cross = jnp.dot(qc, state.astype(BF16), preferred_element_type=F32) * q_decay
        k_dec = (kc.astype(F32) * k_decay).T.astype(BF16)
        kv = jnp.dot(k_dec, vc, preferred_element_type=F32)
        state_ref[...] = state * chunk_decay + jnp.where(same_head, kv, 0.0)
        yield
        o_all = jnp.dot((s * d_intra).astype(BF16), vc, preferred_element_type=F32)
        inner = jnp.zeros((c, w), F32)
        for h in range(RET_HEADS):
            inner = jnp.where(lane_head == h, o_all[h * c:(h + 1) * c], inner)
        o = inner + cross

        y = o * lax.rsqrt(_group_mean_sq(o, bd) + EPS) * ng
        g = gate[sl]
        o_ref[sl, :] = (y * (g * _sigmoid(g))).astype(BF16)


def _merge_ffn_kernel(x_ref, oda_ref, oret_ref, osc_ref, gate_ref, wda_ref, wret_ref, wsc_ref,
                      wout_ref, ng_ref, win_ref, cw_ref, cb_ref, wo_ref,
                      out_ref, h_ref, ext_ref, carry_ref, act_ref, *, blocks_per_seq):
    tm = x_ref.shape[0]
    d = D_MODEL
    y = (gate_ref[:, 0:d].astype(F32)
         * jnp.dot(oda_ref[...], wda_ref[...], preferred_element_type=F32))
    y += (gate_ref[:, d:2 * d].astype(F32)
          * jnp.dot(oret_ref[...], wret_ref[...], preferred_element_type=F32))
    y += (gate_ref[:, 2 * d:3 * d].astype(F32)
          * jnp.dot(osc_ref[...], wsc_ref[...], preferred_element_type=F32))
    x1 = x_ref[...] + jnp.dot(y.astype(BF16), wout_ref[...], preferred_element_type=F32)
    out_ref[...] = x1
    ms = jnp.mean(x1 * x1, axis=-1, keepdims=True)
    h_ref[...] = (x1 * lax.rsqrt(ms + EPS) * ng_ref[...]).astype(BF16)

    @pl.when(pl.program_id(0) % blocks_per_seq == 0)
    def _():
        carry_ref[...] = jnp.zeros_like(carry_ref)

    for c in range(N_FF_CHUNKS):
        sl = slice(c * FF_CHUNK, (c + 1) * FF_CHUNK)
        g = jnp.dot(h_ref[...], win_ref[:, sl], preferred_element_type=F32)
        u = jnp.dot(h_ref[...], win_ref[:, D_FF + c * FF_CHUNK:D_FF + (c + 1) * FF_CHUNK],
                    preferred_element_type=F32)
        ext_ref[0:SUBLANES, :] = carry_ref[:, sl]
        ext_ref[SUBLANES:SUBLANES + tm, :] = g
        gc = (cb_ref[:, sl]
              + cw_ref[0:1, sl] * ext_ref[SUBLANES - 2:SUBLANES - 2 + tm, :]
              + cw_ref[1:2, sl] * ext_ref[SUBLANES - 1:SUBLANES - 1 + tm, :]
              + cw_ref[2:3, sl] * g)
        carry_ref[:, sl] = g[tm - SUBLANES:, :]
        act_ref[:, sl] = (gc * _sigmoid(gc) * u).astype(BF16)

    out_ref[...] += jnp.dot(act_ref[...], wo_ref[...], preferred_element_type=F32)


def _merge_ffn(x2d, seq, layer, oda, oret, osc, gates, wda, wret, wsc, wout, ng, win, cw, cb, wo):
    t = x2d.shape[0]
    tm = ROW_BLOCK
    blocks_per_seq = seq // tm
    row = lambda width: pl.BlockSpec((tm, width), lambda i: (i, 0))
    return pl.pallas_call(
        functools.partial(_merge_ffn_kernel, blocks_per_seq=blocks_per_seq),
        grid=(t // tm,),
        in_specs=[row(D_MODEL), row(DA_V_W), row(RET_V_W), row(SC_WIDTH), row(N_BRANCH * D_MODEL)]
                 + [_resident_layer(a.shape, layer) if a.ndim == 3 else _resident(a.shape)
                    for a in (wda, wret, wsc, wout, ng, win, cw, cb, wo)],
        out_specs=row(D_MODEL),
        out_shape=jax.ShapeDtypeStruct((t, D_MODEL), F32),
        scratch_shapes=[pltpu.VMEM((tm, D_MODEL), BF16),
                        pltpu.VMEM((tm + SUBLANES, FF_CHUNK), F32),
                        pltpu.VMEM((SUBLANES, D_FF), F32),
                        pltpu.VMEM((tm, D_FF), BF16)],
        compiler_params=pltpu.CompilerParams(dimension_semantics=("arbitrary",),
                                             vmem_limit_bytes=VMEM_LIMIT),
        name="merge_ffn",
    )(x2d, oda, oret, osc, gates, wda, wret, wsc, wout, ng, win, cw, cb, wo)


def _rel_bucket(n):
    max_exact = REL_BUCKETS // 2
    nf = jnp.maximum(n, 1).astype(F32)
    large = max_exact + (jnp.log(nf / max_exact) / math.log(REL_MAX_DIST / max_exact)
                         * (REL_BUCKETS - max_exact)).astype(jnp.int32)
    large = jnp.minimum(large, REL_BUCKETS - 1)
    return jnp.where(n < max_exact, n, large)


def _near_bias_tiles(rel_bias, tq):
    hb = tq // 2
    assert tq == 2 * hb and hb >= REL_MAX_DIST
    n_heads = rel_bias.shape[1]
    dist = jnp.arange(2 * hb)
    shifted = (rel_bias[_rel_bucket(dist)] - rel_bias[REL_BUCKETS - 1]) * LOG2E
    by_offset = jnp.concatenate([jnp.full((hb - 1, n_heads), NEG, F32), shifted,
                                 jnp.zeros((1, n_heads), F32)], axis=0).T
    m = by_offset.shape[1]
    skew = jnp.tile(by_offset, (1, hb))[:, :hb * (m - 1)].reshape(n_heads, hb, m - 1)
    near = skew[:, :, hb - 1:2 * hb - 1]
    next_ = skew[:, :, 2 * hb - 1:3 * hb - 1]
    zero = jnp.zeros_like(near)
    masked = jnp.full_like(near, NEG)
    grid2 = lambda rows: jnp.concatenate([jnp.concatenate(r, axis=-1) for r in rows], axis=-2)
    diag = grid2([[near, next_], [masked, near]])
    prev = grid2([[zero, zero], [next_, zero]])
    return jnp.stack([diag, prev, jnp.zeros_like(diag)], axis=1).astype(F32)


def _rotary_tables(seq):
    half = RET_QK_DIM // 2
    inv = ROPE_THETA ** (-jnp.arange(half, dtype=F32) / half)
    ang = jnp.arange(seq).astype(F32)[:, None] * inv[None, :]
    cos, sin = jnp.cos(ang), jnp.sin(ang)
    zero = jnp.zeros_like(sin)
    reps = LANES // RET_QK_DIM
    cos_t = jnp.tile(jnp.concatenate([cos, cos], axis=1), (1, reps))
    sina_t = jnp.tile(jnp.concatenate([-sin, zero], axis=1), (1, reps))
    sinb_t = jnp.tile(jnp.concatenate([zero, sin], axis=1), (1, reps))
    return cos_t, sina_t, sinb_t


@jax.jit
def _forward(x, rel_bias, norm_mix_g, w_in, b_gate, da_q_norm_g, da_k_norm_g, da_lambda,
             da_subln_g, ret_norm_g, sc_conv_w, sc_conv_b, w_branch_da, w_branch_ret,
             w_branch_sc, w_out, norm_ffn_g, w_ffn_in, ffn_conv_w, ffn_conv_b, w_ffn_out):
    b, s, d = x.shape
    depth = w_in.shape[0]
    assert d == D_MODEL and s % ROW_BLOCK == 0 and s % ATT_BLOCK == 0
    nb = _near_bias_tiles(rel_bias.astype(F32), ATT_BLOCK)
    cos_t, sina_t, sinb_t = _rotary_tables(s)
    group = jnp.arange(256) // 64
    bd = (group[:, None] == group[None, :]).astype(BF16)

    w_in, w_branch_da, w_branch_ret, w_branch_sc, w_out, w_ffn_in, w_ffn_out = (
        w.astype(BF16) for w in (w_in, w_branch_da, w_branch_ret, w_branch_sc, w_out, w_ffn_in,
                                 w_ffn_out))
    x2d = x.reshape(b * s, d)
    for l in range(depth):
        lam_init = 0.8 - 0.6 * math.exp(-0.3 * l)
        q, k, vt, oret, osc, gates = _in_proj(
            x2d, s, l, norm_mix_g[l][None, :], w_in, b_gate[l][None, :],
            jnp.tile(da_q_norm_g[l], 256 // DA_QK_DIM)[None, :],
            jnp.tile(da_k_norm_g[l], 256 // DA_QK_DIM)[None, :],
            cos_t, sina_t, sinb_t, sc_conv_w[l], sc_conv_b[l][None, :], bd,
            jnp.tile(ret_norm_g[l], RET_HEADS)[None, :])
        to3 = lambda a: a.reshape(b, s, a.shape[-1])
        oda = _diff_attn(to3(q), to3(k), vt, nb, da_lambda[l].astype(F32),
                         da_subln_g[l][:, None], lam_init)
        x2d = _merge_ffn(
            x2d, s, l, oda.reshape(b * s, DA_V_W), oret, osc, gates,
            w_branch_da, w_branch_ret, w_branch_sc, w_out, norm_ffn_g[l][None, :],
            w_ffn_in, ffn_conv_w[l], ffn_conv_b[l][None, :], w_ffn_out)
    return x2d.reshape(b, s, d)


def kernel(x, rel_bias, norm_mix_g, w_in, b_gate, da_q_norm_g, da_k_norm_g, da_lambda, da_subln_g,
           ret_norm_g, sc_conv_w, sc_conv_b, w_branch_da, w_branch_ret, w_branch_sc, w_out,
           norm_ffn_g, w_ffn_in, ffn_conv_w, ffn_conv_b, w_ffn_out):
    return _forward(x, rel_bias, norm_mix_g, w_in, b_gate, da_q_norm_g, da_k_norm_g, da_lambda,
                    da_subln_g, ret_norm_g, sc_conv_w, sc_conv_b, w_branch_da, w_branch_ret,
                    w_branch_sc, w_out, norm_ffn_g, w_ffn_in, ffn_conv_w, ffn_conv_b, w_ffn_out)
```

```python
import functools
import math

import jax
import jax.numpy as jnp
import numpy as np
from jax import lax
from jax.experimental import pallas as pl
from jax.experimental.pallas import tpu as pltpu

D_MODEL = 1024
DA_HEADS = 4
DA_QK_DIM = 64
DA_V_DIM = 2 * DA_QK_DIM
RET_HEADS = 4
RET_QK_DIM = 64
RET_V_DIM = 64
SC_WIDTH = 256
CONV_W = 3
D_FF = 2816
REL_BUCKETS = 32
REL_MAX_DIST = 128
ROPE_THETA = 10000.0
EPS = 1e-6
N_BRANCH = 3

DA_QK_W = DA_HEADS * 2 * DA_QK_DIM
DA_V_W = DA_HEADS * DA_V_DIM
RET_QK_W = RET_HEADS * RET_QK_DIM
RET_V_W = RET_HEADS * RET_V_DIM

OFF_DA_Q = 0
OFF_DA_K = OFF_DA_Q + DA_QK_W
OFF_DA_V = OFF_DA_K + DA_QK_W
OFF_R_Q = OFF_DA_V + DA_V_W
OFF_R_K = OFF_R_Q + RET_QK_W
OFF_R_V = OFF_R_K + RET_QK_W
OFF_R_G = OFF_R_V + RET_V_W
OFF_SC_B = OFF_R_G + RET_V_W
OFF_SC_C = OFF_SC_B + SC_WIDTH
OFF_SC_X = OFF_SC_C + SC_WIDTH
OFF_GATE = OFF_SC_X + SC_WIDTH
IN_WIDTH = OFF_GATE + N_BRANCH * D_MODEL

LANES = 128
SUBLANES = 8
ROW_BLOCK = 512
ATT_BLOCK = 512
RET_CHUNK = 128
GATE_CHUNK = 512
FF_CHUNK = 256
N_FF_CHUNKS = D_FF // FF_CHUNK
VMEM_LIMIT = 56 * 1024 * 1024
NEG = -1e30
LOG2E = math.log2(math.e)
BF16 = jnp.bfloat16
F32 = jnp.float32


def _resident(shape):
    nd = len(shape)
    return pl.BlockSpec(shape, lambda *_: (0,) * nd, pipeline_mode=pl.Buffered(1))


def _resident_layer(stacked_shape, layer):
    _, rows, cols = stacked_shape
    return pl.BlockSpec((None, rows, cols), lambda *_: (layer, 0, 0), pipeline_mode=pl.Buffered(1))


def _group_mean_sq(y, bd):
    return jnp.dot((y * y).astype(BF16), bd, preferred_element_type=F32) * (1.0 / 64.0)


def _sigmoid(z):
    return 1.0 / (1.0 + jnp.exp(-z))


def _in_proj_kernel(x_ref, g_ref, w_ref, bg_ref, qg_ref, kg_ref, cos_ref, sina_ref, sinb_ref,
                    cw_ref, cb_ref, bd_ref, rn_ref,
                    q_out, k_out, vt_out, ret_out, sc_out, gate_out,
                    h_ref, ext_ref, state_ref, *, blocks_per_seq):
    tm = x_ref.shape[0]

    @pl.when(pl.program_id(0) % blocks_per_seq == 0)
    def _():
        ext_ref[0:SUBLANES, :] = jnp.zeros((SUBLANES, SC_WIDTH), F32)
        state_ref[...] = jnp.zeros_like(state_ref)

    x = x_ref[...]
    ms = jnp.mean(x * x, axis=-1, keepdims=True)
    h_ref[...] = (x * lax.rsqrt(ms + EPS) * g_ref[...]).astype(BF16)
    bd = bd_ref[...]

    def proj(off, width):
        return jnp.dot(h_ref[...], w_ref[:, off:off + width], preferred_element_type=F32)

    halves = range(DA_QK_W // 256)
    yq = [proj(OFF_DA_Q + half * 256, 256) for half in halves]
    yk = [proj(OFF_DA_K + half * 256, 256) for half in halves]
    v = proj(OFF_DA_V, DA_V_W)
    for c in range(tm // ATT_BLOCK):
        vt_out[0, c] = v[c * ATT_BLOCK:(c + 1) * ATT_BLOCK, :].T.astype(BF16)

    def rotary(y):
        outs = []
        for c in range(RET_QK_W // LANES):
            yc = y[:, c * LANES:(c + 1) * LANES]
            outs.append(yc * cos_ref[...]
                        + pltpu.roll(yc, LANES - 32, axis=1) * sina_ref[...]
                        + pltpu.roll(yc, 32, axis=1) * sinb_ref[...])
        return jnp.concatenate(outs, axis=1)

    rq = rotary(proj(OFF_R_Q, RET_QK_W)).astype(BF16)
    rk = (rotary(proj(OFF_R_K, RET_QK_W)) * (RET_QK_DIM ** -0.5)).astype(BF16)

    for half in halves:
        sl = slice(half * 256, (half + 1) * 256)
        q_out[:, sl] = (yq[half] * lax.rsqrt(_group_mean_sq(yq[half], bd) + EPS)
                        * (qg_ref[...] * (DA_QK_DIM ** -0.5 * LOG2E))).astype(BF16)
        k_out[:, sl] = (yk[half] * lax.rsqrt(_group_mean_sq(yk[half], bd) + EPS)
                        * kg_ref[...]).astype(BF16)

    rv = proj(OFF_R_V, RET_V_W).astype(BF16)
    ret_steps = _retention_steps(rq, rk, rv, proj(OFF_R_G, RET_V_W), rn_ref[...], bd, state_ref,
                                 ret_out)

    u = proj(OFF_SC_C, SC_WIDTH) * proj(OFF_SC_X, SC_WIDTH)
    ext_ref[SUBLANES:SUBLANES + tm, :] = u
    conv = (cb_ref[...]
            + cw_ref[0:1, :] * ext_ref[SUBLANES - 2:SUBLANES - 2 + tm, :]
            + cw_ref[1:2, :] * ext_ref[SUBLANES - 1:SUBLANES - 1 + tm, :]
            + cw_ref[2:3, :] * u)
    ext_ref[0:SUBLANES, :] = u[tm - SUBLANES:, :]
    sc_out[...] = (proj(OFF_SC_B, SC_WIDTH) * conv).astype(BF16)

    for c in range(N_BRANCH * D_MODEL // GATE_CHUNK):
        next(ret_steps, None)
        sl = slice(c * GATE_CHUNK, (c + 1) * GATE_CHUNK)
        z = proj(OFF_GATE + c * GATE_CHUNK, GATE_CHUNK) + bg_ref[:, sl]
        gate_out[:, sl] = _sigmoid(z).astype(BF16)
    for _ in ret_steps:
        pass


def _in_proj(x2d, seq, layer, g, w, bg, qg, kg, cos_t, sina_t, sinb_t, cw, cb, bd, rn):
    t = x2d.shape[0]
    tm = ROW_BLOCK
    blocks_per_seq = seq // tm
    row = lambda width: pl.BlockSpec((tm, width), lambda i: (i, 0))
    tab = pl.BlockSpec((tm, LANES), lambda i: (i % blocks_per_seq, 0))
    out_widths = (DA_QK_W, DA_QK_W, RET_V_W, SC_WIDTH, N_BRANCH * D_MODEL)
    out_specs = [row(wd) for wd in out_widths]
    out_shape = [jax.ShapeDtypeStruct((t, wd), BF16) for wd in out_widths]
    out_specs.insert(2, pl.BlockSpec((1, tm // ATT_BLOCK, DA_V_W, ATT_BLOCK),
                                     lambda i: (i // blocks_per_seq, i % blocks_per_seq, 0, 0)))
    out_shape.insert(2, jax.ShapeDtypeStruct((t // seq, seq // ATT_BLOCK, DA_V_W, ATT_BLOCK), BF16))
    return pl.pallas_call(
        functools.partial(_in_proj_kernel, blocks_per_seq=blocks_per_seq),
        grid=(t // tm,),
        in_specs=[row(D_MODEL), _resident(g.shape), _resident_layer(w.shape, layer),
                  _resident(bg.shape),
                  _resident(qg.shape), _resident(kg.shape), tab, tab, tab,
                  _resident(cw.shape), _resident(cb.shape), _resident(bd.shape),
                  _resident(rn.shape)],
        out_specs=out_specs,
        out_shape=out_shape,
        scratch_shapes=[pltpu.VMEM((tm, D_MODEL), BF16),
                        pltpu.VMEM((tm + SUBLANES, SC_WIDTH), F32),
                        pltpu.VMEM((RET_QK_W, RET_V_W), F32)],
        compiler_params=pltpu.CompilerParams(dimension_semantics=("arbitrary",),
                                             vmem_limit_bytes=VMEM_LIMIT),
        name="in_proj",
    )(x2d, g, w, bg, qg, kg, cos_t, sina_t, sinb_t, cw, cb, bd, rn)


def _diff_attn_kernel(q_ref, k_ref, vt_ref, nb_ref, lam_ref, sg_ref, o_ref,
                      s0_ref, s1_ref, acc0_ref, acc1_ref, *, lam_init):
    tq = o_ref.shape[1]
    qi = pl.program_id(2)
    n_q = pl.num_programs(2)

    def masked_q(block):
        q = q_ref[0, pl.ds(pl.multiple_of(block * tq, tq), tq), :]
        lane = lax.broadcasted_iota(jnp.int32, q.shape, 1)
        zero = jnp.zeros_like(q)
        return jnp.where(lane < DA_QK_DIM, q, zero), jnp.where(lane >= DA_QK_DIM, q, zero)

    q_map = masked_q(qi)
    s_refs = (s0_ref, s1_ref)
    acc_refs = (acc0_ref, acc1_ref)

    def put_scores(mp, j, far=False, q_block=qi, q_rows=None):
        start = pl.multiple_of(j * tq, tq)
        q_rows = q_map[mp] if q_rows is None else q_rows
        s = lax.dot_general(k_ref[0, pl.ds(start, tq), :], q_rows, (((1,), (1,)), ((), ())),
                            preferred_element_type=F32)
        s_refs[mp][...] = s if far else s + nb_ref[0, jnp.minimum(q_block - j, 2)]

    def update(mp, j, stats):
        m, l = stats
        m_new = jnp.maximum(m, jnp.max(s_refs[mp][...], axis=0, keepdims=True))
        alpha = jnp.exp2(m - m_new)
        p = jnp.exp2(s_refs[mp][...] - m_new)
        l = alpha * l + jnp.sum(p, axis=0, keepdims=True)
        acc_refs[mp][...] = alpha * acc_refs[mp][...] + jnp.dot(
            vt_ref[0, j], p.astype(BF16), preferred_element_type=F32)
        return m_new, l

    def step(j, stats, far=False):
        put_scores(1, j, far)
        stats0 = update(0, j, stats[0])
        put_scores(0, j + 1, far)
        stats1 = update(1, j, stats[1])
        return stats0, stats1

    init = (jnp.full((1, tq), NEG, F32), jnp.zeros((1, tq), F32))
    acc0_ref[...] = jnp.zeros_like(acc0_ref)
    acc1_ref[...] = jnp.zeros_like(acc1_ref)

    @pl.when(qi == 0)
    def _():
        put_scores(0, 0)

    n_far_pairs = jnp.maximum(qi - 2, 0) // 2

    def far_pair(t, stats):
        return step(2 * t + 1, step(2 * t, stats, far=True), far=True)

    stats = lax.fori_loop(0, n_far_pairs, far_pair, (init, init))
    stats = lax.fori_loop(2 * n_far_pairs, qi, step, stats)
    put_scores(1, qi)
    _, l0 = update(0, qi, stats[0])
    nxt = jnp.minimum(qi + 1, n_q - 1)
    put_scores(0, 0, q_block=nxt, q_rows=masked_q(nxt)[0])
    _, l1 = update(1, qi, stats[1])

    lp = lam_ref[...]
    lam = (jnp.exp(jnp.sum(lp[0:1] * lp[1:2], axis=1, keepdims=True))
           - jnp.exp(jnp.sum(lp[2:3] * lp[3:4], axis=1, keepdims=True)) + lam_init)
    o = acc0_ref[...] * (1.0 / l0) - lam * (acc1_ref[...] * (1.0 / l1))
    ms = jnp.mean(o * o, axis=0, keepdims=True)
    o_ref[0] = (o * lax.rsqrt(ms + EPS) * (sg_ref[...] * (1.0 - lam_init))).T.astype(BF16)


def _diff_attn(q, k, vt, nb, lam_p, sg, lam_init):
    b, s, _ = q.shape
    tq = ATT_BLOCK
    return pl.pallas_call(
        functools.partial(_diff_attn_kernel, lam_init=lam_init),
        grid=(b, DA_HEADS, s // tq),
        in_specs=[pl.BlockSpec((1, s, LANES), lambda bi, h, i: (bi, 0, h)),
                  pl.BlockSpec((1, s, LANES), lambda bi, h, i: (bi, 0, h)),
                  pl.BlockSpec((1, s // tq, DA_V_DIM, tq), lambda bi, h, i: (bi, 0, h, 0)),
                  pl.BlockSpec((1, 3, tq, tq), lambda bi, h, i: (h, 0, 0, 0)),
                  pl.BlockSpec(lam_p.shape, lambda bi, h, i: (0, 0)),
                  pl.BlockSpec(sg.shape, lambda bi, h, i: (0, 0))],
        out_specs=pl.BlockSpec((1, tq, LANES), lambda bi, h, i: (bi, i, h)),
        out_shape=jax.ShapeDtypeStruct((b, s, DA_V_W), BF16),
        scratch_shapes=[pltpu.VMEM((tq, tq), F32), pltpu.VMEM((tq, tq), F32),
                        pltpu.VMEM((DA_V_DIM, tq), F32), pltpu.VMEM((DA_V_DIM, tq), F32)],
        compiler_params=pltpu.CompilerParams(
            dimension_semantics=("arbitrary", "arbitrary", "arbitrary"),
            vmem_limit_bytes=VMEM_LIMIT),
        name="diff_attn",
    )(q, k, vt, nb, lam_p, sg)


def _ret_log_gamma(head):
    lg = [math.log(1.0 - 2.0 ** (-5.0 - h)) for h in range(RET_HEADS)]
    out = jnp.full(head.shape, lg[RET_HEADS - 1], F32)
    for h in range(RET_HEADS - 2, -1, -1):
        out = jnp.where(head == h, lg[h], out)
    return out


def _retention_steps(q, k, v, gate, ng, bd, state_ref, o_ref):
    c = RET_CHUNK
    w = RET_QK_W
    n_chunks = q.shape[0] // c
    head_shift = RET_QK_DIM.bit_length() - 1
    chunk_shift = c.bit_length() - 1
    assert 1 << head_shift == RET_QK_DIM == RET_V_DIM and 1 << chunk_shift == c
    lane_head = lax.broadcasted_iota(jnp.int32, (c, w), 1) >> head_shift
    row = lax.broadcasted_iota(jnp.int32, (c, w), 0).astype(F32)
    lg_lane = _ret_log_gamma(lane_head)
    q_decay = jnp.exp(lg_lane * (row + 1.0))
    k_decay = jnp.exp(lg_lane * (c - 1.0 - row))
    chunk_decay = jnp.exp(lg_lane[0:1, :] * float(c))
    srow = lax.broadcasted_iota(jnp.int32, (RET_HEADS * c, c), 0)
    scol = lax.broadcasted_iota(jnp.int32, (RET_HEADS * c, c), 1)
    diff = ((srow & (c - 1)) - scol).astype(F32)
    d_intra = jnp.where(diff >= 0,
                        jnp.exp(_ret_log_gamma(srow >> chunk_shift) * jnp.maximum(diff, 0.0)), 0.0)
    st_row_head = lax.broadcasted_iota(jnp.int32, (w, w), 0) >> head_shift
    st_col_head = lax.broadcasted_iota(jnp.int32, (w, w), 1) >> head_shift
    same_head = st_row_head == st_col_head

    for n in range(n_chunks):
        sl = slice(n * c, (n + 1) * c)
        qc, kc, vc = q[sl], k[sl], v[sl]
        zero = jnp.zeros_like(qc)
        qs = jnp.concatenate([jnp.where(lane_head == h, qc, zero) for h in range(RET_HEADS)], axis=0)
        s = lax.dot_general(qs, kc, (((1,), (1,)), ((), ())), preferred_element_type=F32)
        state = state_ref[...]
        cross = jnp.dot(qc, state.astype(BF16), preferred_element_type=F32) * q_decay
        k_dec = (kc.astype(F32) * k_decay).T.astype(BF16)
        kv = jnp.dot(k_dec, vc, preferred_element_type=F32)
        state_ref[...] = state * chunk_decay + jnp.where(same_head, kv, 0.0)
        yield
        o_all = jnp.dot((s * d_intra).astype(BF16), vc, preferred_element_type=F32)
        inner = jnp.zeros((c, w), F32)
        for h in range(RET_HEADS):
            inner = jnp.where(lane_head == h, o_all[h * c:(h + 1) * c], inner)
        o = inner + cross

        y = o * lax.rsqrt(_group_mean_sq(o, bd) + EPS) * ng
        g = gate[sl]
        o_ref[sl, :] = (y * (g * _sigmoid(g))).astype(BF16)


def _merge_ffn_kernel(x_ref, oda_ref, oret_ref, osc_ref, gate_ref, wda_ref, wret_ref, wsc_ref,
                      wout_ref, ng_ref, win_ref, cw_ref, cb_ref, wo_ref,
                      out_ref, h_ref, ext_ref, carry_ref, act_ref, *, blocks_per_seq):
    tm = x_ref.shape[0]
    d = D_MODEL
    y = (gate_ref[:, 0:d].astype(F32)
         * jnp.dot(oda_ref[...], wda_ref[...], preferred_element_type=F32))
    y += (gate_ref[:, d:2 * d].astype(F32)
          * jnp.dot(oret_ref[...], wret_ref[...], preferred_element_type=F32))
    y += (gate_ref[:, 2 * d:3 * d].astype(F32)
          * jnp.dot(osc_ref[...], wsc_ref[...], preferred_element_type=F32))
    x1 = x_ref[...] + jnp.dot(y.astype(BF16), wout_ref[...], preferred_element_type=F32)
    out_ref[...] = x1
    ms = jnp.mean(x1 * x1, axis=-1, keepdims=True)
    h_ref[...] = (x1 * lax.rsqrt(ms + EPS) * ng_ref[...]).astype(BF16)

    @pl.when(pl.program_id(0) % blocks_per_seq == 0)
    def _():
        carry_ref[...] = jnp.zeros_like(carry_ref)

    for c in range(N_FF_CHUNKS):
        sl = slice(c * FF_CHUNK, (c + 1) * FF_CHUNK)
        g = jnp.dot(h_ref[...], win_ref[:, sl], preferred_element_type=F32)
        u = jnp.dot(h_ref[...], win_ref[:, D_FF + c * FF_CHUNK:D_FF + (c + 1) * FF_CHUNK],
                    preferred_element_type=F32)
        ext_ref[0:SUBLANES, :] = carry_ref[:, sl]
        ext_ref[SUBLANES:SUBLANES + tm, :] = g
        gc = (cb_ref[:, sl]
              + cw_ref[0:1, sl] * ext_ref[SUBLANES - 2:SUBLANES - 2 + tm, :]
              + cw_ref[1:2, sl] * ext_ref[SUBLANES - 1:SUBLANES - 1 + tm, :]
              + cw_ref[2:3, sl] * g)
        carry_ref[:, sl] = g[tm - SUBLANES:, :]
        act_ref[:, sl] = (gc * _sigmoid(gc) * u).astype(BF16)

    out_ref[...] += jnp.dot(act_ref[...], wo_ref[...], preferred_element_type=F32)


def _merge_ffn(x2d, seq, layer, oda, oret, osc, gates, wda, wret, wsc, wout, ng, win, cw, cb, wo):
    t = x2d.shape[0]
    tm = ROW_BLOCK
    blocks_per_seq = seq // tm
    row = lambda width: pl.BlockSpec((tm, width), lambda i: (i, 0))
    return pl.pallas_call(
        functools.partial(_merge_ffn_kernel, blocks_per_seq=blocks_per_seq),
        grid=(t // tm,),
        in_specs=[row(D_MODEL), row(DA_V_W), row(RET_V_W), row(SC_WIDTH), row(N_BRANCH * D_MODEL)]
                 + [_resident_layer(a.shape, layer) if a.ndim == 3 else _resident(a.shape)
                    for a in (wda, wret, wsc, wout, ng, win, cw, cb, wo)],
        out_specs=row(D_MODEL),
        out_shape=jax.ShapeDtypeStruct((t, D_MODEL), F32),
        scratch_shapes=[pltpu.VMEM((tm, D_MODEL), BF16),
                        pltpu.VMEM((tm + SUBLANES, FF_CHUNK), F32),
                        pltpu.VMEM((SUBLANES, D_FF), F32),
                        pltpu.VMEM((tm, D_FF), BF16)],
        compiler_params=pltpu.CompilerParams(dimension_semantics=("arbitrary",),
                                             vmem_limit_bytes=VMEM_LIMIT),
        name="merge_ffn",
    )(x2d, oda, oret, osc, gates, wda, wret, wsc, wout, ng, win, cw, cb, wo)


def _rel_bucket(n):
    max_exact = REL_BUCKETS // 2
    nf = jnp.maximum(n, 1).astype(F32)
    large = max_exact + (jnp.log(nf / max_exact) / math.log(REL_MAX_DIST / max_exact)
                         * (REL_BUCKETS - max_exact)).astype(jnp.int32)
    large = jnp.minimum(large, REL_BUCKETS - 1)
    return jnp.where(n < max_exact, n, large)


def _near_bias_tiles(rel_bias, tq):
    hb = tq // 2
    assert tq == 2 * hb and hb >= REL_MAX_DIST
    n_heads = rel_bias.shape[1]
    dist = jnp.arange(2 * hb)
    shifted = (rel_bias[_rel_bucket(dist)] - rel_bias[REL_BUCKETS - 1]) * LOG2E
    by_offset = jnp.concatenate([jnp.full((hb - 1, n_heads), NEG, F32), shifted,
                                 jnp.zeros((1, n_heads), F32)], axis=0).T
    m = by_offset.shape[1]
    skew = jnp.tile(by_offset, (1, hb))[:, :hb * (m - 1)].reshape(n_heads, hb, m - 1)
    near = skew[:, :, hb - 1:2 * hb - 1]
    next_ = skew[:, :, 2 * hb - 1:3 * hb - 1]
    zero = jnp.zeros_like(near)
    masked = jnp.full_like(near, NEG)
    grid2 = lambda rows: jnp.concatenate([jnp.concatenate(r, axis=-1) for r in rows], axis=-2)
    diag = grid2([[near, next_], [masked, near]])
    prev = grid2([[zero, zero], [next_, zero]])
    return jnp.stack([diag, prev, jnp.zeros_like(diag)], axis=1).astype(F32)


def _rotary_tables(seq):
    half = RET_QK_DIM // 2
    inv = ROPE_THETA ** (-jnp.arange(half, dtype=F32) / half)
    ang = jnp.arange(seq).astype(F32)[:, None] * inv[None, :]
    cos, sin = jnp.cos(ang), jnp.sin(ang)
    zero = jnp.zeros_like(sin)
    reps = LANES // RET_QK_DIM
    cos_t = jnp.tile(jnp.concatenate([cos, cos], axis=1), (1, reps))
    sina_t = jnp.tile(jnp.concatenate([-sin, zero], axis=1), (1, reps))
    sinb_t = jnp.tile(jnp.concatenate([zero, sin], axis=1), (1, reps))
    return cos_t, sina_t, sinb_t


@jax.jit
def _forward(x, rel_bias, norm_mix_g, w_in, b_gate, da_q_norm_g, da_k_norm_g, da_lambda,
             da_subln_g, ret_norm_g, sc_conv_w, sc_conv_b, w_branch_da, w_branch_ret,
             w_branch_sc, w_out, norm_ffn_g, w_ffn_in, ffn_conv_w, ffn_conv_b, w_ffn_out):
    b, s, d = x.shape
    depth = w_in.shape[0]
    assert d == D_MODEL and s % ROW_BLOCK == 0 and s % ATT_BLOCK == 0
    nb = _near_bias_tiles(rel_bias.astype(F32), ATT_BLOCK)
    cos_t, sina_t, sinb_t = _rotary_tables(s)
    group = jnp.arange(256) // 64
    bd = (group[:, None] == group[None, :]).astype(BF16)

    w_in, w_branch_da, w_branch_ret, w_branch_sc, w_out, w_ffn_in, w_ffn_out = (
        w.astype(BF16) for w in (w_in, w_branch_da, w_branch_ret, w_branch_sc, w_out, w_ffn_in,
                                 w_ffn_out))
    x2d = x.reshape(b * s, d)
    for l in range(depth):
        lam_init = 0.8 - 0.6 * math.exp(-0.3 * l)
        q, k, vt, oret, osc, gates = _in_proj(
            x2d, s, l, norm_mix_g[l][None, :], w_in, b_gate[l][None, :],
            jnp.tile(da_q_norm_g[l], 256 // DA_QK_DIM)[None, :],
            jnp.tile(da_k_norm_g[l], 256 // DA_QK_DIM)[None, :],
            cos_t, sina_t, sinb_t, sc_conv_w[l], sc_conv_b[l][None, :], bd,
            jnp.tile(ret_norm_g[l], RET_HEADS)[None, :])
        to3 = lambda a: a.reshape(b, s, a.shape[-1])
        oda = _diff_attn(to3(q), to3(k), vt, nb, da_lambda[l].astype(F32),
                         da_subln_g[l][:, None], lam_init)
        x2d = _merge_ffn(
            x2d, s, l, oda.reshape(b * s, DA_V_W), oret, osc, gates,
            w_branch_da, w_branch_ret, w_branch_sc, w_out, norm_ffn_g[l][None, :],
            w_ffn_in, ffn_conv_w[l], ffn_conv_b[l][None, :], w_ffn_out)
    return x2d.reshape(b, s, d)


def kernel(x, rel_bias, norm_mix_g, w_in, b_gate, da_q_norm_g, da_k_norm_g, da_lambda, da_subln_g,
           ret_norm_g, sc_conv_w, sc_conv_b, w_branch_da, w_branch_ret, w_branch_sc, w_out,
           norm_ffn_g, w_ffn_in, ffn_conv_w, ffn_conv_b, w_ffn_out):
    return _forward(x, rel_bias, norm_mix_g, w_in, b_gate, da_q_norm_g, da_k_norm_g, da_lambda,
                    da_subln_g, ret_norm_g, sc_conv_w, sc_conv_b, w_branch_da, w_branch_ret,
                    w_branch_sc, w_out, norm_ffn_g, w_ffn_in, ffn_conv_w, ffn_conv_b, w_ffn_out)
```

```python
import functools
import math

import jax
import jax.numpy as jnp
import numpy as np
from jax import lax
from jax.experimental import pallas as pl
from jax.experimental.pallas import tpu as pltpu

D_MODEL = 1024
DA_HEADS = 4
DA_QK_DIM = 64
DA_V_DIM = 2 * DA_QK_DIM
RET_HEADS = 4
RET_QK_DIM = 64
RET_V_DIM = 64
SC_WIDTH = 256
CONV_W = 3
D_FF = 2816
REL_BUCKETS = 32
REL_MAX_DIST = 128
ROPE_THETA = 10000.0
EPS = 1e-6
N_BRANCH = 3

DA_QK_W = DA_HEADS * 2 * DA_QK_DIM
DA_V_W = DA_HEADS * DA_V_DIM
RET_QK_W = RET_HEADS * RET_QK_DIM
RET_V_W = RET_HEADS * RET_V_DIM

OFF_DA_Q = 0
OFF_DA_K = OFF_DA_Q + DA_QK_W
OFF_DA_V = OFF_DA_K + DA_QK_W
OFF_R_Q = OFF_DA_V + DA_V_W
OFF_R_K = OFF_R_Q + RET_QK_W
OFF_R_V = OFF_R_K + RET_QK_W
OFF_R_G = OFF_R_V + RET_V_W
OFF_SC_B = OFF_R_G + RET_V_W
OFF_SC_C = OFF_SC_B + SC_WIDTH
OFF_SC_X = OFF_SC_C + SC_WIDTH
OFF_GATE = OFF_SC_X + SC_WIDTH
IN_WIDTH = OFF_GATE + N_BRANCH * D_MODEL

LANES = 128
SUBLANES = 8
ROW_BLOCK = 512
ATT_BLOCK = 512
RET_CHUNK = 128
GATE_CHUNK = 512
FF_CHUNK = 256
N_FF_CHUNKS = D_FF // FF_CHUNK
VMEM_LIMIT = 56 * 1024 * 1024
NEG = -1e30
LOG2E = math.log2(math.e)
BF16 = jnp.bfloat16
F32 = jnp.float32


def _resident(shape):
    nd = len(shape)
    return pl.BlockSpec(shape, lambda *_: (0,) * nd, pipeline_mode=pl.Buffered(1))


def _resident_layer(stacked_shape, layer):
    _, rows, cols = stacked_shape
    return pl.BlockSpec((None, rows, cols), lambda *_: (layer, 0, 0), pipeline_mode=pl.Buffered(1))


def _group_mean_sq(y, bd):
    return jnp.dot((y * y).astype(BF16), bd, preferred_element_type=F32) * (1.0 / 64.0)


def _sigmoid(z):
    return 1.0 / (1.0 + jnp.exp(-z))


def _in_proj_kernel(x_ref, g_ref, w_ref, bg_ref, qg_ref, kg_ref, cos_ref, sina_ref, sinb_ref,
                    cw_ref, cb_ref, bd_ref, rn_ref,
                    q_out, k_out, vt_out, ret_out, sc_out, gate_out,
                    h_ref, ext_ref, state_ref, *, blocks_per_seq):
    tm = x_ref.shape[0]

    @pl.when(pl.program_id(0) % blocks_per_seq == 0)
    def _():
        ext_ref[0:SUBLANES, :] = jnp.zeros((SUBLANES, SC_WIDTH), F32)
        state_ref[...] = jnp.zeros_like(state_ref)

    x = x_ref[...]
    ms = jnp.mean(x * x, axis=-1, keepdims=True)
    h_ref[...] = (x * lax.rsqrt(ms + EPS) * g_ref[...]).astype(BF16)
    bd = bd_ref[...]

    def proj(off, width):
        return jnp.dot(h_ref[...], w_ref[:, off:off + width], preferred_element_type=F32)

    halves = range(DA_QK_W // 256)
    yq = [proj(OFF_DA_Q + half * 256, 256) for half in halves]
    yk = [proj(OFF_DA_K + half * 256, 256) for half in halves]
    v = proj(OFF_DA_V, DA_V_W)
    for c in range(tm // ATT_BLOCK):
        vt_out[0, c] = v[c * ATT_BLOCK:(c + 1) * ATT_BLOCK, :].T.astype(BF16)

    def rotary(y):
        outs = []
        for c in range(RET_QK_W // LANES):
            yc = y[:, c * LANES:(c + 1) * LANES]
            outs.append(yc * cos_ref[...]
                        + pltpu.roll(yc, LANES - 32, axis=1) * sina_ref[...]
                        + pltpu.roll(yc, 32, axis=1) * sinb_ref[...])
        return jnp.concatenate(outs, axis=1)

    rq = rotary(proj(OFF_R_Q, RET_QK_W)).astype(BF16)
    rk = (rotary(proj(OFF_R_K, RET_QK_W)) * (RET_QK_DIM ** -0.5)).astype(BF16)

    for half in halves:
        sl = slice(half * 256, (half + 1) * 256)
        q_out[:, sl] = (yq[half] * lax.rsqrt(_group_mean_sq(yq[half], bd) + EPS)
                        * (qg_ref[...] * (DA_QK_DIM ** -0.5 * LOG2E))).astype(BF16)
        k_out[:, sl] = (yk[half] * lax.rsqrt(_group_mean_sq(yk[half], bd) + EPS)
                        * kg_ref[...]).astype(BF16)

    rv = proj(OFF_R_V, RET_V_W).astype(BF16)
    ret_steps = _retention_steps(rq, rk, rv, proj(OFF_R_G, RET_V_W), rn_ref[...], bd, state_ref,
                                 ret_out)

    u = proj(OFF_SC_C, SC_WIDTH) * proj(OFF_SC_X, SC_WIDTH)
    ext_ref[SUBLANES:SUBLANES + tm, :] = u
    conv = (cb_ref[...]
            + cw_ref[0:1, :] * ext_ref[SUBLANES - 2:SUBLANES - 2 + tm, :]
            + cw_ref[1:2, :] * ext_ref[SUBLANES - 1:SUBLANES - 1 + tm, :]
            + cw_ref[2:3, :] * u)
    ext_ref[0:SUBLANES, :] = u[tm - SUBLANES:, :]
    sc_out[...] = (proj(OFF_SC_B, SC_WIDTH) * conv).astype(BF16)

    for c in range(N_BRANCH * D_MODEL // GATE_CHUNK):
        next(ret_steps, None)
        sl = slice(c * GATE_CHUNK, (c + 1) * GATE_CHUNK)
        z = proj(OFF_GATE + c * GATE_CHUNK, GATE_CHUNK) + bg_ref[:, sl]
        gate_out[:, sl] = _sigmoid(z).astype(BF16)
    for _ in ret_steps:
        pass


def _in_proj(x2d, seq, layer, g, w, bg, qg, kg, cos_t, sina_t, sinb_t, cw, cb, bd, rn):
    t = x2d.shape[0]
    tm = ROW_BLOCK
    blocks_per_seq = seq // tm
    row = lambda width: pl.BlockSpec((tm, width), lambda i: (i, 0))
    tab = pl.BlockSpec((tm, LANES), lambda i: (i % blocks_per_seq, 0))
    out_widths = (DA_QK_W, DA_QK_W, RET_V_W, SC_WIDTH, N_BRANCH * D_MODEL)
    out_specs = [row(wd) for wd in out_widths]
    out_shape = [jax.ShapeDtypeStruct((t, wd), BF16) for wd in out_widths]
    out_specs.insert(2, pl.BlockSpec((1, tm // ATT_BLOCK, DA_V_W, ATT_BLOCK),
                                     lambda i: (i // blocks_per_seq, i % blocks_per_seq, 0, 0)))
    out_shape.insert(2, jax.ShapeDtypeStruct((t // seq, seq // ATT_BLOCK, DA_V_W, ATT_BLOCK), BF16))
    return pl.pallas_call(
        functools.partial(_in_proj_kernel, blocks_per_seq=blocks_per_seq),
        grid=(t // tm,),
        in_specs=[row(D_MODEL), _resident(g.shape), _resident_layer(w.shape, layer),
                  _resident(bg.shape),
                  _resident(qg.shape), _resident(kg.shape), tab, tab, tab,
                  _resident(cw.shape), _resident(cb.shape), _resident(bd.shape),
                  _resident(rn.shape)],
        out_specs=out_specs,
        out_shape=out_shape,
        scratch_shapes=[pltpu.VMEM((tm, D_MODEL), BF16),
                        pltpu.VMEM((tm + SUBLANES, SC_WIDTH), F32),
                        pltpu.VMEM((RET_QK_W, RET_V_W), F32)],
        compiler_params=pltpu.CompilerParams(dimension_semantics=("arbitrary",),
                                             vmem_limit_bytes=VMEM_LIMIT),
        name="in_proj",
    )(x2d, g, w, bg, qg, kg, cos_t, sina_t, sinb_t, cw, cb, bd, rn)


def _diff_attn_kernel(q_ref, k_ref, vt_ref, nb_ref, lam_ref, sg_ref, o_ref,
                      s0_ref, s1_ref, acc0_ref, acc1_ref, *, lam_init):
    tq = o_ref.shape[1]
    qi = pl.program_id(2)
    n_q = pl.num_programs(2)

    def masked_q(block):
        q = q_ref[0, pl.ds(pl.multiple_of(block * tq, tq), tq), :]
        lane = lax.broadcasted_iota(jnp.int32, q.shape, 1)
        zero = jnp.zeros_like(q)
        return jnp.where(lane < DA_QK_DIM, q, zero), jnp.where(lane >= DA_QK_DIM, q, zero)

    q_map = masked_q(qi)
    s_refs = (s0_ref, s1_ref)
    acc_refs = (acc0_ref, acc1_ref)

    def put_scores(mp, j, far=False, q_block=qi, q_rows=None):
        start = pl.multiple_of(j * tq, tq)
        q_rows = q_map[mp] if q_rows is None else q_rows
        s = lax.dot_general(k_ref[0, pl.ds(start, tq), :], q_rows, (((1,), (1,)), ((), ())),
                            preferred_element_type=F32)
        s_refs[mp][...] = s if far else s + nb_ref[0, jnp.minimum(q_block - j, 2)]

    def update(mp, j, stats):
        m, l = stats
        m_new = jnp.maximum(m, jnp.max(s_refs[mp][...], axis=0, keepdims=True))
        alpha = jnp.exp2(m - m_new)
        p = jnp.exp2(s_refs[mp][...] - m_new)
        l = alpha * l + jnp.sum(p, axis=0, keepdims=True)
        acc_refs[mp][...] = alpha * acc_refs[mp][...] + jnp.dot(
            vt_ref[0, j], p.astype(BF16), preferred_element_type=F32)
        return m_new, l

    def step(j, stats, far=False):
        put_scores(1, j, far)
        stats0 = update(0, j, stats[0])
        put_scores(0, j + 1, far)
        stats1 = update(1, j, stats[1])
        return stats0, stats1

    init = (jnp.full((1, tq), NEG, F32), jnp.zeros((1, tq), F32))
    acc0_ref[...] = jnp.zeros_like(acc0_ref)
    acc1_ref[...] = jnp.zeros_like(acc1_ref)

    @pl.when(qi == 0)
    def _():
        put_scores(0, 0)

    n_far = jnp.maximum(qi - 2, 0)

    def far_pair(t, stats):
        return step(2 * t + 1, step(2 * t, stats, far=True), far=True)

    stats = lax.fori_loop(0, lax.shift_right_logical(n_far, 1), far_pair, (init, init))
    stats = lax.cond((n_far & 1) == 1, lambda st: step(n_far - 1, st, far=True), lambda st: st, stats)
    stats = lax.cond(
        qi >= 2, lambda st: step(qi - 1, step(qi - 2, st)),
        lambda st: lax.cond(qi == 1, lambda s1: step(0, s1), lambda s1: s1, st), stats)
    put_scores(1, qi)
    _, l0 = update(0, qi, stats[0])
    nxt = jnp.minimum(qi + 1, n_q - 1)
    put_scores(0, 0, q_block=nxt, q_rows=masked_q(nxt)[0])
    _, l1 = update(1, qi, stats[1])

    lp = lam_ref[...]
    lam = (jnp.exp(jnp.sum(lp[0:1] * lp[1:2], axis=1, keepdims=True))
           - jnp.exp(jnp.sum(lp[2:3] * lp[3:4], axis=1, keepdims=True)) + lam_init)
    o = acc0_ref[...] * (1.0 / l0) - lam * (acc1_ref[...] * (1.0 / l1))
    ms = jnp.mean(o * o, axis=0, keepdims=True)
    o_ref[0] = (o * lax.rsqrt(ms + EPS) * (sg_ref[...] * (1.0 - lam_init))).T.astype(BF16)


def _diff_attn(q, k, vt, nb, lam_p, sg, lam_init):
    b, s, _ = q.shape
    tq = ATT_BLOCK
    return pl.pallas_call(
        functools.partial(_diff_attn_kernel, lam_init=lam_init),
        grid=(b, DA_HEADS, s // tq),
        in_specs=[pl.BlockSpec((1, s, LANES), lambda bi, h, i: (bi, 0, h)),
                  pl.BlockSpec((1, s, LANES), lambda bi, h, i: (bi, 0, h)),
                  pl.BlockSpec((1, s // tq, DA_V_DIM, tq), lambda bi, h, i: (bi, 0, h, 0)),
                  pl.BlockSpec((1, 3, tq, tq), lambda bi, h, i: (h, 0, 0, 0)),
                  pl.BlockSpec(lam_p.shape, lambda bi, h, i: (0, 0)),
                  pl.BlockSpec(sg.shape, lambda bi, h, i: (0, 0))],
        out_specs=pl.BlockSpec((1, tq, LANES), lambda bi, h, i: (bi, i, h)),
        out_shape=jax.ShapeDtypeStruct((b, s, DA_V_W), BF16),
        scratch_shapes=[pltpu.VMEM((tq, tq), F32), pltpu.VMEM((tq, tq), F32),
                        pltpu.VMEM((DA_V_DIM, tq), F32), pltpu.VMEM((DA_V_DIM, tq), F32)],
        compiler_params=pltpu.CompilerParams(
            dimension_semantics=("arbitrary", "arbitrary", "arbitrary"),
            vmem_limit_bytes=VMEM_LIMIT),
        name="diff_attn",
    )(q, k, vt, nb, lam_p, sg)


def _ret_log_gamma(head):
    lg = [math.log(1.0 - 2.0 ** (-5.0 - h)) for h in range(RET_HEADS)]
    out = jnp.full(head.shape, lg[RET_HEADS - 1], F32)
    for h in range(RET_HEADS - 2, -1, -1):
        out = jnp.where(head == h, lg[h], out)
    return out


def _retention_steps(q, k, v, gate, ng, bd, state_ref, o_ref):
    c = RET_CHUNK
    w = RET_QK_W
    n_chunks = q.shape[0] // c
    head_shift = RET_QK_DIM.bit_length() - 1
    chunk_shift = c.bit_length() - 1
    assert 1 << head_shift == RET_QK_DIM == RET_V_DIM and 1 << chunk_shift == c
    lane_head = lax.broadcasted_iota(jnp.int32, (c, w), 1) >> head_shift
    row = lax.broadcasted_iota(jnp.int32, (c, w), 0).astype(F32)
    lg_lane = _ret_log_gamma(lane_head)
    q_decay = jnp.exp(lg_lane * (row + 1.0))
    k_decay = jnp.exp(lg_lane * (c - 1.0 - row))
    chunk_decay = jnp.exp(lg_lane[0:1, :] * float(c))
    srow = lax.broadcasted_iota(jnp.int32, (RET_HEADS * c, c), 0)
    scol = lax.broadcasted_iota(jnp.int32, (RET_HEADS * c, c), 1)
    diff = ((srow & (c - 1)) - scol).astype(F32)
    d_intra = jnp.where(diff >= 0,
                        jnp.exp(_ret_log_gamma(srow >> chunk_shift) * jnp.maximum(diff, 0.0)), 0.0)
    st_row_head = lax.broadcasted_iota(jnp.int32, (w, w), 0) >> head_shift
    st_col_head = lax.broadcasted_iota(jnp.int32, (w, w), 1) >> head_shift
    same_head = st_row_head == st_col_head

    for n in range(n_chunks):
        sl = slice(n * c, (n + 1) * c)
        qc, kc, vc = q[sl], k[sl], v[sl]
        zero = jnp.zeros_like(qc)
        qs = jnp.concatenate([jnp.where(lane_head == h, qc, zero) for h in range(RET_HEADS)], axis=0)
        s = lax.dot_general(qs, kc, (((1,), (1,)), ((), ())), preferred_element_type=F32)
        state = state_ref[...]
        cross = jnp.dot(qc, state.astype(BF16), preferred_element_type=F32) * q_decay
        k_dec = (kc.astype(F32) * k_decay).T.astype(BF16)
        kv = jnp.dot(k_dec, vc, preferred_element_type=F32)
        state_ref[...] = state * chunk_decay + jnp.where(same_head, kv, 0.0)
        yield
        o_all = jnp.dot((s * d_intra).astype(BF16), vc, preferred_element_type=F32)
        inner = jnp.zeros((c, w), F32)
        for h in range(RET_HEADS):
            inner = jnp.where(lane_head == h, o_all[h * c:(h + 1) * c], inner)
        o = inner + cross

        y = o * lax.rsqrt(_group_mean_sq(o, bd) + EPS) * ng
        g = gate[sl]
        o_ref[sl, :] = (y * (g * _sigmoid(g))).astype(BF16)


def _merge_ffn_kernel(x_ref, oda_ref, oret_ref, osc_ref, gate_ref, wda_ref, wret_ref, wsc_ref,
                      wout_ref, ng_ref, win_ref, cw_ref, cb_ref, wo_ref,
                      out_ref, h_ref, ext_ref, carry_ref, act_ref, *, blocks_per_seq):
    tm = x_ref.shape[0]
    d = D_MODEL
    y = (gate_ref[:, 0:d].astype(F32)
         * jnp.dot(oda_ref[...], wda_ref[...], preferred_element_type=F32))
    y += (gate_ref[:, d:2 * d].astype(F32)
          * jnp.dot(oret_ref[...], wret_ref[...], preferred_element_type=F32))
    y += (gate_ref[:, 2 * d:3 * d].astype(F32)
          * jnp.dot(osc_ref[...], wsc_ref[...], preferred_element_type=F32))
    x1 = x_ref[...] + jnp.dot(y.astype(BF16), wout_ref[...], preferred_element_type=F32)
    out_ref[...] = x1
    ms = jnp.mean(x1 * x1, axis=-1, keepdims=True)
    h_ref[...] = (x1 * lax.rsqrt(ms + EPS) * ng_ref[...]).astype(BF16)

    @pl.when(pl.program_id(0) % blocks_per_seq == 0)
    def _():
        carry_ref[...] = jnp.zeros_like(carry_ref)

    for c in range(N_FF_CHUNKS):
        sl = slice(c * FF_CHUNK, (c + 1) * FF_CHUNK)
        g = jnp.dot(h_ref[...], win_ref[:, sl], preferred_element_type=F32)
        u = jnp.dot(h_ref[...], win_ref[:, D_FF + c * FF_CHUNK:D_FF + (c + 1) * FF_CHUNK],
                    preferred_element_type=F32)
        ext_ref[0:SUBLANES, :] = carry_ref[:, sl]
        ext_ref[SUBLANES:SUBLANES + tm, :] = g
        gc = (cb_ref[:, sl]
              + cw_ref[0:1, sl] * ext_ref[SUBLANES - 2:SUBLANES - 2 + tm, :]
              + cw_ref[1:2, sl] * ext_ref[SUBLANES - 1:SUBLANES - 1 + tm, :]
              + cw_ref[2:3, sl] * g)
        carry_ref[:, sl] = g[tm - SUBLANES:, :]
        act_ref[:, sl] = (gc * _sigmoid(gc) * u).astype(BF16)

    out_ref[...] += jnp.dot(act_ref[...], wo_ref[...], preferred_element_type=F32)


def _merge_ffn(x2d, seq, layer, oda, oret, osc, gates, wda, wret, wsc, wout, ng, win, cw, cb, wo):
    t = x2d.shape[0]
    tm = ROW_BLOCK
    blocks_per_seq = seq // tm
    row = lambda width: pl.BlockSpec((tm, width), lambda i: (i, 0))
    return pl.pallas_call(
        functools.partial(_merge_ffn_kernel, blocks_per_seq=blocks_per_seq),
        grid=(t // tm,),
        in_specs=[row(D_MODEL), row(DA_V_W), row(RET_V_W), row(SC_WIDTH), row(N_BRANCH * D_MODEL)]
                 + [_resident_layer(a.shape, layer) if a.ndim == 3 else _resident(a.shape)
                    for a in (wda, wret, wsc, wout, ng, win, cw, cb, wo)],
        out_specs=row(D_MODEL),
        out_shape=jax.ShapeDtypeStruct((t, D_MODEL), F32),
        scratch_shapes=[pltpu.VMEM((tm, D_MODEL), BF16),
                        pltpu.VMEM((tm + SUBLANES, FF_CHUNK), F32),
                        pltpu.VMEM((SUBLANES, D_FF), F32),
                        pltpu.VMEM((tm, D_FF), BF16)],
        compiler_params=pltpu.CompilerParams(dimension_semantics=("arbitrary",),
                                             vmem_limit_bytes=VMEM_LIMIT),
        name="merge_ffn",
    )(x2d, oda, oret, osc, gates, wda, wret, wsc, wout, ng, win, cw, cb, wo)


def _rel_bucket(n):
    max_exact = REL_BUCKETS // 2
    nf = jnp.maximum(n, 1).astype(F32)
    large = max_exact + (jnp.log(nf / max_exact) / math.log(REL_MAX_DIST / max_exact)
                         * (REL_BUCKETS - max_exact)).astype(jnp.int32)
    large = jnp.minimum(large, REL_BUCKETS - 1)
    return jnp.where(n < max_exact, n, large)


def _near_bias_tiles(rel_bias, tq):
    hb = tq // 2
    assert tq == 2 * hb and hb >= REL_MAX_DIST
    n_heads = rel_bias.shape[1]
    dist = jnp.arange(2 * hb)
    shifted = (rel_bias[_rel_bucket(dist)] - rel_bias[REL_BUCKETS - 1]) * LOG2E
    by_offset = jnp.concatenate([jnp.full((hb - 1, n_heads), NEG, F32), shifted,
                                 jnp.zeros((1, n_heads), F32)], axis=0).T
    m = by_offset.shape[1]
    skew = jnp.tile(by_offset, (1, hb))[:, :hb * (m - 1)].reshape(n_heads, hb, m - 1)
    near = skew[:, :, hb - 1:2 * hb - 1]
    next_ = skew[:, :, 2 * hb - 1:3 * hb - 1]
    zero = jnp.zeros_like(near)
    masked = jnp.full_like(near, NEG)
    grid2 = lambda rows: jnp.concatenate([jnp.concatenate(r, axis=-1) for r in rows], axis=-2)
    diag = grid2([[near, next_], [masked, near]])
    prev = grid2([[zero, zero], [next_, zero]])
    return jnp.stack([diag, prev, jnp.zeros_like(diag)], axis=1).astype(F32)


def _rotary_tables(seq):
    half = RET_QK_DIM // 2
    inv = ROPE_THETA ** (-jnp.arange(half, dtype=F32) / half)
    ang = jnp.arange(seq).astype(F32)[:, None] * inv[None, :]
    cos, sin = jnp.cos(ang), jnp.sin(ang)
    zero = jnp.zeros_like(sin)
    reps = LANES // RET_QK_DIM
    cos_t = jnp.tile(jnp.concatenate([cos, cos], axis=1), (1, reps))
    sina_t = jnp.tile(jnp.concatenate([-sin, zero], axis=1), (1, reps))
    sinb_t = jnp.tile(jnp.concatenate([zero, sin], axis=1), (1, reps))
    return cos_t, sina_t, sinb_t


@jax.jit
def _forward(x, rel_bias, norm_mix_g, w_in, b_gate, da_q_norm_g, da_k_norm_g, da_lambda,
             da_subln_g, ret_norm_g, sc_conv_w, sc_conv_b, w_branch_da, w_branch_ret,
             w_branch_sc, w_out, norm_ffn_g, w_ffn_in, ffn_conv_w, ffn_conv_b, w_ffn_out):
    b, s, d = x.shape
    depth = w_in.shape[0]
    assert d == D_MODEL and s % ROW_BLOCK == 0 and s % ATT_BLOCK == 0
    nb = _near_bias_tiles(rel_bias.astype(F32), ATT_BLOCK)
    cos_t, sina_t, sinb_t = _rotary_tables(s)
    group = jnp.arange(256) // 64
    bd = (group[:, None] == group[None, :]).astype(BF16)

    w_in, w_branch_da, w_branch_ret, w_branch_sc, w_out, w_ffn_in, w_ffn_out = (
        w.astype(BF16) for w in (w_in, w_branch_da, w_branch_ret, w_branch_sc, w_out, w_ffn_in,
                                 w_ffn_out))
    x2d = x.reshape(b * s, d)
    for l in range(depth):
        lam_init = 0.8 - 0.6 * math.exp(-0.3 * l)
        q, k, vt, oret, osc, gates = _in_proj(
            x2d, s, l, norm_mix_g[l][None, :], w_in, b_gate[l][None, :],
            jnp.tile(da_q_norm_g[l], 256 // DA_QK_DIM)[None, :],
            jnp.tile(da_k_norm_g[l], 256 // DA_QK_DIM)[None, :],
            cos_t, sina_t, sinb_t, sc_conv_w[l], sc_conv_b[l][None, :], bd,
            jnp.tile(ret_norm_g[l], RET_HEADS)[None, :])
        to3 = lambda a: a.reshape(b, s, a.shape[-1])
        oda = _diff_attn(to3(q), to3(k), vt, nb, da_lambda[l].astype(F32),
                         da_subln_g[l][:, None], lam_init)
        x2d = _merge_ffn(
            x2d, s, l, oda.reshape(b * s, DA_V_W), oret, osc, gates,
            w_branch_da, w_branch_ret, w_branch_sc, w_out, norm_ffn_g[l][None, :],
            w_ffn_in, ffn_conv_w[l], ffn_conv_b[l][None, :], w_ffn_out)
    return x2d.reshape(b, s, d)


def kernel(x, rel_bias, norm_mix_g, w_in, b_gate, da_q_norm_g, da_k_norm_g, da_lambda, da_subln_g,
           ret_norm_g, sc_conv_w, sc_conv_b, w_branch_da, w_branch_ret, w_branch_sc, w_out,
           norm_ffn_g, w_ffn_in, ffn_conv_w, ffn_conv_b, w_ffn_out):
    return _forward(x, rel_bias, norm_mix_g, w_in, b_gate, da_q_norm_g, da_k_norm_g, da_lambda,
                    da_subln_g, ret_norm_g, sc_conv_w, sc_conv_b, w_branch_da, w_branch_ret,
                    w_branch_sc, w_out, norm_ffn_g, w_ffn_in, ffn_conv_w, ffn_conv_b, w_ffn_out)
```

```python
import functools
import math

import jax
import jax.numpy as jnp
import numpy as np
from jax import lax
from jax.experimental import pallas as pl
from jax.experimental.pallas import tpu as pltpu

D_MODEL = 1024
DA_HEADS = 4
DA_QK_DIM = 64
DA_V_DIM = 2 * DA_QK_DIM
RET_HEADS = 4
RET_QK_DIM = 64
RET_V_DIM = 64
SC_WIDTH = 256
CONV_W = 3
D_FF = 2816
REL_BUCKETS = 32
REL_MAX_DIST = 128
ROPE_THETA = 10000.0
EPS = 1e-6
N_BRANCH = 3

DA_QK_W = DA_HEADS * 2 * DA_QK_DIM
DA_V_W = DA_HEADS * DA_V_DIM
RET_QK_W = RET_HEADS * RET_QK_DIM
RET_V_W = RET_HEADS * RET_V_DIM

OFF_DA_Q = 0
OFF_DA_K = OFF_DA_Q + DA_QK_W
OFF_DA_V = OFF_DA_K + DA_QK_W
OFF_R_Q = OFF_DA_V + DA_V_W
OFF_R_K = OFF_R_Q + RET_QK_W
OFF_R_V = OFF_R_K + RET_QK_W
OFF_R_G = OFF_R_V + RET_V_W
OFF_SC_B = OFF_R_G + RET_V_W
OFF_SC_C = OFF_SC_B + SC_WIDTH
OFF_SC_X = OFF_SC_C + SC_WIDTH
OFF_GATE = OFF_SC_X + SC_WIDTH
IN_WIDTH = OFF_GATE + N_BRANCH * D_MODEL

LANES = 128
SUBLANES = 8
ROW_BLOCK = 512
ATT_BLOCK = 512
RET_CHUNK = 128
GATE_CHUNK = 512
FF_CHUNK = 256
N_FF_CHUNKS = D_FF // FF_CHUNK
VMEM_LIMIT = 56 * 1024 * 1024
NEG = -1e30
LOG2E = math.log2(math.e)
BF16 = jnp.bfloat16
F32 = jnp.float32


def _resident(shape):
    nd = len(shape)
    return pl.BlockSpec(shape, lambda *_: (0,) * nd, pipeline_mode=pl.Buffered(1))


def _resident_layer(stacked_shape, layer):
    _, rows, cols = stacked_shape
    return pl.BlockSpec((None, rows, cols), lambda *_: (layer, 0, 0), pipeline_mode=pl.Buffered(1))


def _group_mean_sq(y, bd):
    return jnp.dot((y * y).astype(BF16), bd, preferred_element_type=F32) * (1.0 / 64.0)


def _sigmoid(z):
    return 1.0 / (1.0 + jnp.exp(-z))


def _in_proj_kernel(x_ref, g_ref, w_ref, bg_ref, qg_ref, kg_ref, cos_ref, sina_ref, sinb_ref,
                    cw_ref, cb_ref, bd_ref, rn_ref,
                    q_out, k_out, vt_out, ret_out, sc_out, gate_out,
                    h_ref, ext_ref, state_ref, *, blocks_per_seq):
    tm = x_ref.shape[0]

    @pl.when(pl.program_id(0) % blocks_per_seq == 0)
    def _():
        ext_ref[0:SUBLANES, :] = jnp.zeros((SUBLANES, SC_WIDTH), F32)
        state_ref[...] = jnp.zeros_like(state_ref)

    x = x_ref[...]
    ms = jnp.mean(x * x, axis=-1, keepdims=True)
    h_ref[...] = (x * lax.rsqrt(ms + EPS) * g_ref[...]).astype(BF16)
    bd = bd_ref[...]

    def proj(off, width):
        return jnp.dot(h_ref[...], w_ref[:, off:off + width], preferred_element_type=F32)

    halves = range(DA_QK_W // 256)
    yq = [proj(OFF_DA_Q + half * 256, 256) for half in halves]
    yk = [proj(OFF_DA_K + half * 256, 256) for half in halves]
    v = proj(OFF_DA_V, DA_V_W)
    for c in range(tm // ATT_BLOCK):
        vt_out[0, c] = v[c * ATT_BLOCK:(c + 1) * ATT_BLOCK, :].T.astype(BF16)

    def rotary(y):
        outs = []
        for c in range(RET_QK_W // LANES):
            yc = y[:, c * LANES:(c + 1) * LANES]
            outs.append(yc * cos_ref[...]
                        + pltpu.roll(yc, LANES - 32, axis=1) * sina_ref[...]
                        + pltpu.roll(yc, 32, axis=1) * sinb_ref[...])
        return jnp.concatenate(outs, axis=1)

    rq = rotary(proj(OFF_R_Q, RET_QK_W)).astype(BF16)
    rk = (rotary(proj(OFF_R_K, RET_QK_W)) * (RET_QK_DIM ** -0.5)).astype(BF16)

    for half in halves:
        sl = slice(half * 256, (half + 1) * 256)
        q_out[:, sl] = (yq[half] * lax.rsqrt(_group_mean_sq(yq[half], bd) + EPS)
                        * (qg_ref[...] * (DA_QK_DIM ** -0.5 * LOG2E))).astype(BF16)
        k_out[:, sl] = (yk[half] * lax.rsqrt(_group_mean_sq(yk[half], bd) + EPS)
                        * kg_ref[...]).astype(BF16)

    rv = proj(OFF_R_V, RET_V_W).astype(BF16)
    ret_steps = _retention_steps(rq, rk, rv, proj(OFF_R_G, RET_V_W), rn_ref[...], bd, state_ref,
                                 ret_out)

    u = proj(OFF_SC_C, SC_WIDTH) * proj(OFF_SC_X, SC_WIDTH)
    ext_ref[SUBLANES:SUBLANES + tm, :] = u
    conv = (cb_ref[...]
            + cw_ref[0:1, :] * ext_ref[SUBLANES - 2:SUBLANES - 2 + tm, :]
            + cw_ref[1:2, :] * ext_ref[SUBLANES - 1:SUBLANES - 1 + tm, :]
            + cw_ref[2:3, :] * u)
    ext_ref[0:SUBLANES, :] = u[tm - SUBLANES:, :]
    sc_out[...] = (proj(OFF_SC_B, SC_WIDTH) * conv).astype(BF16)

    for c in range(N_BRANCH * D_MODEL // GATE_CHUNK):
        next(ret_steps, None)
        sl = slice(c * GATE_CHUNK, (c + 1) * GATE_CHUNK)
        z = proj(OFF_GATE + c * GATE_CHUNK, GATE_CHUNK) + bg_ref[:, sl]
        gate_out[:, sl] = _sigmoid(z).astype(BF16)
    for _ in ret_steps:
        pass


def _in_proj(x2d, seq, layer, g, w, bg, qg, kg, cos_t, sina_t, sinb_t, cw, cb, bd, rn):
    t = x2d.shape[0]
    tm = ROW_BLOCK
    blocks_per_seq = seq // tm
    row = lambda width: pl.BlockSpec((tm, width), lambda i: (i, 0))
    tab = pl.BlockSpec((tm, LANES), lambda i: (i % blocks_per_seq, 0))
    out_widths = (DA_QK_W, DA_QK_W, RET_V_W, SC_WIDTH, N_BRANCH * D_MODEL)
    out_specs = [row(wd) for wd in out_widths]
    out_shape = [jax.ShapeDtypeStruct((t, wd), BF16) for wd in out_widths]
    out_specs.insert(2, pl.BlockSpec((1, tm // ATT_BLOCK, DA_V_W, ATT_BLOCK),
                                     lambda i: (i // blocks_per_seq, i % blocks_per_seq, 0, 0)))
    out_shape.insert(2, jax.ShapeDtypeStruct((t // seq, seq // ATT_BLOCK, DA_V_W, ATT_BLOCK), BF16))
    return pl.pallas_call(
        functools.partial(_in_proj_kernel, blocks_per_seq=blocks_per_seq),
        grid=(t // tm,),
        in_specs=[row(D_MODEL), _resident(g.shape), _resident_layer(w.shape, layer),
                  _resident(bg.shape),
                  _resident(qg.shape), _resident(kg.shape), tab, tab, tab,
                  _resident(cw.shape), _resident(cb.shape), _resident(bd.shape),
                  _resident(rn.shape)],
        out_specs=out_specs,
        out_shape=out_shape,
        scratch_shapes=[pltpu.VMEM((tm, D_MODEL), BF16),
                        pltpu.VMEM((tm + SUBLANES, SC_WIDTH), F32),
                        pltpu.VMEM((RET_QK_W, RET_V_W), F32)],
        compiler_params=pltpu.CompilerParams(dimension_semantics=("arbitrary",),
                                             vmem_limit_bytes=VMEM_LIMIT),
        name="in_proj",
    )(x2d, g, w, bg, qg, kg, cos_t, sina_t, sinb_t, cw, cb, bd, rn)


def _diff_attn_kernel(q_ref, k_ref, vt_ref, nb_ref, lam_ref, sg_ref, o_ref,
                      s0_ref, s1_ref, acc0_ref, acc1_ref, *, lam_init):
    tq = o_ref.shape[1]
    qi = pl.program_id(2)
    n_q = pl.num_programs(2)

    def masked_q(block):
        q = q_ref[0, pl.ds(pl.multiple_of(block * tq, tq), tq), :]
        lane = lax.broadcasted_iota(jnp.int32, q.shape, 1)
        zero = jnp.zeros_like(q)
        return jnp.where(lane < DA_QK_DIM, q, zero), jnp.where(lane >= DA_QK_DIM, q, zero)

    q_map = masked_q(qi)
    s_refs = (s0_ref, s1_ref)
    acc_refs = (acc0_ref, acc1_ref)

    def put_scores(mp, j, far=False, q_block=qi, q_rows=None):
        start = pl.multiple_of(j * tq, tq)
        q_rows = q_map[mp] if q_rows is None else q_rows
        s = lax.dot_general(k_ref[0, pl.ds(start, tq), :], q_rows, (((1,), (1,)), ((), ())),
                            preferred_element_type=F32)
        s_refs[mp][...] = s if far else s + nb_ref[0, jnp.minimum(q_block - j, 2)]

    def update(mp, j, stats):
        m, l = stats
        m_new = jnp.maximum(m, jnp.max(s_refs[mp][...], axis=0, keepdims=True))
        alpha = jnp.exp2(m - m_new)
        p = jnp.exp2(s_refs[mp][...] - m_new)
        l = alpha * l + jnp.sum(p, axis=0, keepdims=True)
        acc_refs[mp][...] = alpha * acc_refs[mp][...] + jnp.dot(
            vt_ref[0, j], p.astype(BF16), preferred_element_type=F32)
        return m_new, l

    def step(j, stats, far=False):
        put_scores(1, j, far)
        stats0 = update(0, j, stats[0])
        put_scores(0, j + 1, far)
        stats1 = update(1, j, stats[1])
        return stats0, stats1

    init = (jnp.full((1, tq), NEG, F32), jnp.zeros((1, tq), F32))
    acc0_ref[...] = jnp.zeros_like(acc0_ref)
    acc1_ref[...] = jnp.zeros_like(acc1_ref)

    @pl.when(qi == 0)
    def _():
        put_scores(0, 0)

    n_far = jnp.maximum(qi - 2, 0)

    def far_pair(t, stats):
        return step(2 * t + 1, step(2 * t, stats, far=True), far=True)

    stats = lax.fori_loop(0, lax.shift_right_logical(n_far, 1), far_pair, (init, init))
    stats = lax.cond((n_far & 1) == 1, lambda st: step(n_far - 1, st, far=True), lambda st: st, stats)
    def finish(stats):
        put_scores(1, qi)
        _, l0 = update(0, qi, stats[0])
        nxt = jnp.minimum(qi + 1, n_q - 1)
        put_scores(0, 0, q_block=nxt, q_rows=masked_q(nxt)[0])
        _, l1 = update(1, qi, stats[1])

        lp = lam_ref[...]
        lam = (jnp.exp(jnp.sum(lp[0:1] * lp[1:2], axis=1, keepdims=True))
               - jnp.exp(jnp.sum(lp[2:3] * lp[3:4], axis=1, keepdims=True)) + lam_init)
        o = acc0_ref[...] * (1.0 / l0) - lam * (acc1_ref[...] * (1.0 / l1))
        ms = jnp.mean(o * o, axis=0, keepdims=True)
        o_ref[0] = (o * lax.rsqrt(ms + EPS) * (sg_ref[...] * (1.0 - lam_init))).T.astype(BF16)
        return 0

    lax.cond(
        qi >= 2, lambda st: finish(step(qi - 1, step(qi - 2, st))),
        lambda st: lax.cond(qi == 1, lambda s1: finish(step(0, s1)), finish, st), stats)


def _diff_attn(q, k, vt, nb, lam_p, sg, lam_init):
    b, s, _ = q.shape
    tq = ATT_BLOCK
    return pl.pallas_call(
        functools.partial(_diff_attn_kernel, lam_init=lam_init),
        grid=(b, DA_HEADS, s // tq),
        in_specs=[pl.BlockSpec((1, s, LANES), lambda bi, h, i: (bi, 0, h)),
                  pl.BlockSpec((1, s, LANES), lambda bi, h, i: (bi, 0, h)),
                  pl.BlockSpec((1, s // tq, DA_V_DIM, tq), lambda bi, h, i: (bi, 0, h, 0)),
                  pl.BlockSpec((1, 3, tq, tq), lambda bi, h, i: (h, 0, 0, 0)),
                  pl.BlockSpec(lam_p.shape, lambda bi, h, i: (0, 0)),
                  pl.BlockSpec(sg.shape, lambda bi, h, i: (0, 0))],
        out_specs=pl.BlockSpec((1, tq, LANES), lambda bi, h, i: (bi, i, h)),
        out_shape=jax.ShapeDtypeStruct((b, s, DA_V_W), BF16),
        scratch_shapes=[pltpu.VMEM((tq, tq), F32), pltpu.VMEM((tq, tq), F32),
                        pltpu.VMEM((DA_V_DIM, tq), F32), pltpu.VMEM((DA_V_DIM, tq), F32)],
        compiler_params=pltpu.CompilerParams(
            dimension_semantics=("arbitrary", "arbitrary", "arbitrary"),
            vmem_limit_bytes=VMEM_LIMIT),
        name="diff_attn",
    )(q, k, vt, nb, lam_p, sg)


def _ret_log_gamma(head):
    lg = [math.log(1.0 - 2.0 ** (-5.0 - h)) for h in range(RET_HEADS)]
    out = jnp.full(head.shape, lg[RET_HEADS - 1], F32)
    for h in range(RET_HEADS - 2, -1, -1):
        out = jnp.where(head == h, lg[h], out)
    return out


def _retention_steps(q, k, v, gate, ng, bd, state_ref, o_ref):
    c = RET_CHUNK
    w = RET_QK_W
    n_chunks = q.shape[0] // c
    head_shift = RET_QK_DIM.bit_length() - 1
    chunk_shift = c.bit_length() - 1
    assert 1 << head_shift == RET_QK_DIM == RET_V_DIM and 1 << chunk_shift == c
    lane_head = lax.broadcasted_iota(jnp.int32, (c, w), 1) >> head_shift
    row = lax.broadcasted_iota(jnp.int32, (c, w), 0).astype(F32)
    lg_lane = _ret_log_gamma(lane_head)
    q_decay = jnp.exp(lg_lane * (row + 1.0))
    k_decay = jnp.exp(lg_lane * (c - 1.0 - row))
    chunk_decay = jnp.exp(lg_lane[0:1, :] * float(c))
    srow = lax.broadcasted_iota(jnp.int32, (RET_HEADS * c, c), 0)
    scol = lax.broadcasted_iota(jnp.int32, (RET_HEADS * c, c), 1)
    diff = ((srow & (c - 1)) - scol).astype(F32)
    d_intra = jnp.where(diff >= 0,
                        jnp.exp(_ret_log_gamma(srow >> chunk_shift) * jnp.maximum(diff, 0.0)), 0.0)
    st_row_head = lax.broadcasted_iota(jnp.int32, (w, w), 0) >> head_shift
    st_col_head = lax.broadcasted_iota(jnp.int32, (w, w), 1) >> head_shift
    same_head = st_row_head == st_col_head

    for n in range(n_chunks):
        sl = slice(n * c, (n + 1) * c)
        qc, kc, vc = q[sl], k[sl], v[sl]
        zero = jnp.zeros_like(qc)
        qs = jnp.concatenate([jnp.where(lane_head == h, qc, zero) for h in range(RET_HEADS)], axis=0)
        s = lax.dot_general(qs, kc, (((1,), (1,)), ((), ())), preferred_element_type=F32)
        state = state_ref[...]
        cross = jnp.dot(qc, state.astype(BF16), preferred_element_type=F32) * q_decay
        k_dec = (kc.astype(F32) * k_decay).T.astype(BF16)
        kv = jnp.dot(k_dec, vc, preferred_element_type=F32)
        state_ref[...] = state * chunk_decay + jnp.where(same_head, kv, 0.0)
        yield
        o_all = jnp.dot((s * d_intra).astype(BF16), vc, preferred_element_type=F32)
        inner = jnp.zeros((c, w), F32)
        for h in range(RET_HEADS):
            inner = jnp.where(lane_head == h, o_all[h * c:(h + 1) * c], inner)
        o = inner + cross

        y = o * lax.rsqrt(_group_mean_sq(o, bd) + EPS) * ng
        g = gate[sl]
        o_ref[sl, :] = (y * (g * _sigmoid(g))).astype(BF16)


def _merge_ffn_kernel(x_ref, oda_ref, oret_ref, osc_ref, gate_ref, wda_ref, wret_ref, wsc_ref,
                      wout_ref, ng_ref, win_ref, cw_ref, cb_ref, wo_ref,
                      out_ref, h_ref, ext_ref, carry_ref, act_ref, *, blocks_per_seq):
    tm = x_ref.shape[0]
    d = D_MODEL
    y = (gate_ref[:, 0:d].astype(F32)
         * jnp.dot(oda_ref[...], wda_ref[...], preferred_element_type=F32))
    y += (gate_ref[:, d:2 * d].astype(F32)
          * jnp.dot(oret_ref[...], wret_ref[...], preferred_element_type=F32))
    y += (gate_ref[:, 2 * d:3 * d].astype(F32)
          * jnp.dot(osc_ref[...], wsc_ref[...], preferred_element_type=F32))
    x1 = x_ref[...] + jnp.dot(y.astype(BF16), wout_ref[...], preferred_element_type=F32)
    out_ref[...] = x1
    ms = jnp.mean(x1 * x1, axis=-1, keepdims=True)
    h_ref[...] = (x1 * lax.rsqrt(ms + EPS) * ng_ref[...]).astype(BF16)

    @pl.when(pl.program_id(0) % blocks_per_seq == 0)
    def _():
        carry_ref[...] = jnp.zeros_like(carry_ref)

    for c in range(N_FF_CHUNKS):
        sl = slice(c * FF_CHUNK, (c + 1) * FF_CHUNK)
        g = jnp.dot(h_ref[...], win_ref[:, sl], preferred_element_type=F32)
        u = jnp.dot(h_ref[...], win_ref[:, D_FF + c * FF_CHUNK:D_FF + (c + 1) * FF_CHUNK],
                    preferred_element_type=F32)
        ext_ref[0:SUBLANES, :] = carry_ref[:, sl]
        ext_ref[SUBLANES:SUBLANES + tm, :] = g
        gc = (cb_ref[:, sl]
              + cw_ref[0:1, sl] * ext_ref[SUBLANES - 2:SUBLANES - 2 + tm, :]
              + cw_ref[1:2, sl] * ext_ref[SUBLANES - 1:SUBLANES - 1 + tm, :]
              + cw_ref[2:3, sl] * g)
        carry_ref[:, sl] = g[tm - SUBLANES:, :]
        act_ref[:, sl] = (gc * _sigmoid(gc) * u).astype(BF16)

    out_ref[...] += jnp.dot(act_ref[...], wo_ref[...], preferred_element_type=F32)


def _merge_ffn(x2d, seq, layer, oda, oret, osc, gates, wda, wret, wsc, wout, ng, win, cw, cb, wo):
    t = x2d.shape[0]
    tm = ROW_BLOCK
    blocks_per_seq = seq // tm
    row = lambda width: pl.BlockSpec((tm, width), lambda i: (i, 0))
    return pl.pallas_call(
        functools.partial(_merge_ffn_kernel, blocks_per_seq=blocks_per_seq),
        grid=(t // tm,),
        in_specs=[row(D_MODEL), row(DA_V_W), row(RET_V_W), row(SC_WIDTH), row(N_BRANCH * D_MODEL)]
                 + [_resident_layer(a.shape, layer) if a.ndim == 3 else _resident(a.shape)
                    for a in (wda, wret, wsc, wout, ng, win, cw, cb, wo)],
        out_specs=row(D_MODEL),
        out_shape=jax.ShapeDtypeStruct((t, D_MODEL), F32),
        scratch_shapes=[pltpu.VMEM((tm, D_MODEL), BF16),
                        pltpu.VMEM((tm + SUBLANES, FF_CHUNK), F32),
                        pltpu.VMEM((SUBLANES, D_FF), F32),
                        pltpu.VMEM((tm, D_FF), BF16)],
        compiler_params=pltpu.CompilerParams(dimension_semantics=("arbitrary",),
                                             vmem_limit_bytes=VMEM_LIMIT),
        name="merge_ffn",
    )(x2d, oda, oret, osc, gates, wda, wret, wsc, wout, ng, win, cw, cb, wo)


def _rel_bucket(n):
    max_exact = REL_BUCKETS // 2
    nf = jnp.maximum(n, 1).astype(F32)
    large = max_exact + (jnp.log(nf / max_exact) / math.log(REL_MAX_DIST / max_exact)
                         * (REL_BUCKETS - max_exact)).astype(jnp.int32)
    large = jnp.minimum(large, REL_BUCKETS - 1)
    return jnp.where(n < max_exact, n, large)


def _near_bias_tiles(rel_bias, tq):
    hb = tq // 2
    assert tq == 2 * hb and hb >= REL_MAX_DIST
    n_heads = rel_bias.shape[1]
    dist = jnp.arange(2 * hb)
    shifted = (rel_bias[_rel_bucket(dist)] - rel_bias[REL_BUCKETS - 1]) * LOG2E
    by_offset = jnp.concatenate([jnp.full((hb - 1, n_heads), NEG, F32), shifted,
                                 jnp.zeros((1, n_heads), F32)], axis=0).T
    m = by_offset.shape[1]
    skew = jnp.tile(by_offset, (1, hb))[:, :hb * (m - 1)].reshape(n_heads, hb, m - 1)
    near = skew[:, :, hb - 1:2 * hb - 1]
    next_ = skew[:, :, 2 * hb - 1:3 * hb - 1]
    zero = jnp.zeros_like(near)
    masked = jnp.full_like(near, NEG)
    grid2 = lambda rows: jnp.concatenate([jnp.concatenate(r, axis=-1) for r in rows], axis=-2)
    diag = grid2([[near, next_], [masked, near]])
    prev = grid2([[zero, zero], [next_, zero]])
    return jnp.stack([diag, prev, jnp.zeros_like(diag)], axis=1).astype(F32)


def _rotary_tables(seq):
    half = RET_QK_DIM // 2
    inv = ROPE_THETA ** (-jnp.arange(half, dtype=F32) / half)
    ang = jnp.arange(seq).astype(F32)[:, None] * inv[None, :]
    cos, sin = jnp.cos(ang), jnp.sin(ang)
    zero = jnp.zeros_like(sin)
    reps = LANES // RET_QK_DIM
    cos_t = jnp.tile(jnp.concatenate([cos, cos], axis=1), (1, reps))
    sina_t = jnp.tile(jnp.concatenate([-sin, zero], axis=1), (1, reps))
    sinb_t = jnp.tile(jnp.concatenate([zero, sin], axis=1), (1, reps))
    return cos_t, sina_t, sinb_t


@jax.jit
def _forward(x, rel_bias, norm_mix_g, w_in, b_gate, da_q_norm_g, da_k_norm_g, da_lambda,
             da_subln_g, ret_norm_g, sc_conv_w, sc_conv_b, w_branch_da, w_branch_ret,
             w_branch_sc, w_out, norm_ffn_g, w_ffn_in, ffn_conv_w, ffn_conv_b, w_ffn_out):
    b, s, d = x.shape
    depth = w_in.shape[0]
    assert d == D_MODEL and s % ROW_BLOCK == 0 and s % ATT_BLOCK == 0
    nb = _near_bias_tiles(rel_bias.astype(F32), ATT_BLOCK)
    cos_t, sina_t, sinb_t = _rotary_tables(s)
    group = jnp.arange(256) // 64
    bd = (group[:, None] == group[None, :]).astype(BF16)

    w_in, w_branch_da, w_branch_ret, w_branch_sc, w_out, w_ffn_in, w_ffn_out = (
        w.astype(BF16) for w in (w_in, w_branch_da, w_branch_ret, w_branch_sc, w_out, w_ffn_in,
                                 w_ffn_out))
    x2d = x.reshape(b * s, d)
    for l in range(depth):
        lam_init = 0.8 - 0.6 * math.exp(-0.3 * l)
        q, k, vt, oret, osc, gates = _in_proj(
            x2d, s, l, norm_mix_g[l][None, :], w_in, b_gate[l][None, :],
            jnp.tile(da_q_norm_g[l], 256 // DA_QK_DIM)[None, :],
            jnp.tile(da_k_norm_g[l], 256 // DA_QK_DIM)[None, :],
            cos_t, sina_t, sinb_t, sc_conv_w[l], sc_conv_b[l][None, :], bd,
            jnp.tile(ret_norm_g[l], RET_HEADS)[None, :])
        to3 = lambda a: a.reshape(b, s, a.shape[-1])
        oda = _diff_attn(to3(q), to3(k), vt, nb, da_lambda[l].astype(F32),
                         da_subln_g[l][:, None], lam_init)
        x2d = _merge_ffn(
            x2d, s, l, oda.reshape(b * s, DA_V_W), oret, osc, gates,
            w_branch_da, w_branch_ret, w_branch_sc, w_out, norm_ffn_g[l][None, :],
            w_ffn_in, ffn_conv_w[l], ffn_conv_b[l][None, :], w_ffn_out)
    return x2d.reshape(b, s, d)


def kernel(x, rel_bias, norm_mix_g, w_in, b_gate, da_q_norm_g, da_k_norm_g, da_lambda, da_subln_g,
           ret_norm_g, sc_conv_w, sc_conv_b, w_branch_da, w_branch_ret, w_branch_sc, w_out,
           norm_ffn_g, w_ffn_in, ffn_conv_w, ffn_conv_b, w_ffn_out):
    return _forward(x, rel_bias, norm_mix_g, w_in, b_gate, da_q_norm_g, da_k_norm_g, da_lambda,
                    da_subln_g, ret_norm_g, sc_conv_w, sc_conv_b, w_branch_da, w_branch_ret,
                    w_branch_sc, w_out, norm_ffn_g, w_ffn_in, ffn_conv_w, ffn_conv_b, w_ffn_out)
```

```python
import functools
import math

import jax
import jax.numpy as jnp
from jax import lax
from jax.experimental import pallas as pl
from jax.experimental.pallas import tpu as pltpu

D_MODEL = 1024
DA_HEADS = 4
DA_QK_DIM = 64
DA_V_DIM = 2 * DA_QK_DIM
RET_HEADS = 4
RET_QK_DIM = 64
RET_V_DIM = 64
SC_WIDTH = 256
CONV_W = 3
D_FF = 2816
REL_BUCKETS = 32
REL_MAX_DIST = 128
ROPE_THETA = 10000.0
EPS = 1e-6
N_BRANCH = 3

DA_QK_W = DA_HEADS * 2 * DA_QK_DIM
DA_V_W = DA_HEADS * DA_V_DIM
RET_QK_W = RET_HEADS * RET_QK_DIM
RET_V_W = RET_HEADS * RET_V_DIM

OFF_DA_Q = 0
OFF_DA_K = OFF_DA_Q + DA_QK_W
OFF_DA_V = OFF_DA_K + DA_QK_W
OFF_R_Q = OFF_DA_V + DA_V_W
OFF_R_K = OFF_R_Q + RET_QK_W
OFF_R_V = OFF_R_K + RET_QK_W
OFF_R_G = OFF_R_V + RET_V_W
OFF_SC_B = OFF_R_G + RET_V_W
OFF_SC_C = OFF_SC_B + SC_WIDTH
OFF_SC_X = OFF_SC_C + SC_WIDTH
OFF_GATE = OFF_SC_X + SC_WIDTH
IN_WIDTH = OFF_GATE + N_BRANCH * D_MODEL

LANES = 128
MXU_WIDTH = 256
NORM_GROUP = 64
SUBLANES = 8
ROW_BLOCK = 512
ATT_BLOCK = 512
RET_CHUNK = 128
GATE_CHUNK = 512
FF_CHUNK = 256
N_FF_CHUNKS = D_FF // FF_CHUNK
VMEM_LIMIT = 56 * 1024 * 1024
NEG = -1e30
LOG2E = math.log2(math.e)
BF16 = jnp.bfloat16
F32 = jnp.float32


def _resident(shape):
    nd = len(shape)
    return pl.BlockSpec(shape, lambda *_: (0,) * nd, pipeline_mode=pl.Buffered(1))


def _resident_layer(stacked_shape, layer):
    _, rows, cols = stacked_shape
    return pl.BlockSpec((None, rows, cols), lambda *_: (layer, 0, 0), pipeline_mode=pl.Buffered(1))


def _group_mean_sq(y, bd):
    return jnp.dot((y * y).astype(BF16), bd, preferred_element_type=F32) * (1.0 / NORM_GROUP)


def _sigmoid(z):
    return 1.0 / (1.0 + jnp.exp(-z))


def _in_proj_kernel(x_ref, g_ref, w_ref, bg_ref, qg_ref, kg_ref, cos_ref, sina_ref, sinb_ref,
                    cw_ref, cb_ref, bd_ref, rn_ref,
                    q_out, k_out, vt_out, ret_out, sc_out, gate_out,
                    h_ref, ext_ref, state_ref, *, blocks_per_seq):
    tm = x_ref.shape[0]

    @pl.when(pl.program_id(0) % blocks_per_seq == 0)
    def _():
        ext_ref[0:SUBLANES, :] = jnp.zeros((SUBLANES, SC_WIDTH), F32)
        state_ref[...] = jnp.zeros_like(state_ref)

    x = x_ref[...]
    ms = jnp.mean(x * x, axis=-1, keepdims=True)
    h_ref[...] = (x * lax.rsqrt(ms + EPS) * g_ref[...]).astype(BF16)
    bd = bd_ref[...]

    def proj(off, width):
        return jnp.dot(h_ref[...], w_ref[:, off:off + width], preferred_element_type=F32)

    halves = range(DA_QK_W // MXU_WIDTH)
    yq = [proj(OFF_DA_Q + half * MXU_WIDTH, MXU_WIDTH) for half in halves]
    yk = [proj(OFF_DA_K + half * MXU_WIDTH, MXU_WIDTH) for half in halves]
    v = proj(OFF_DA_V, DA_V_W)
    for c in range(tm // ATT_BLOCK):
        vt_out[0, c] = v[c * ATT_BLOCK:(c + 1) * ATT_BLOCK, :].T.astype(BF16)

    def rotary(y):
        outs = []
        for c in range(RET_QK_W // LANES):
            yc = y[:, c * LANES:(c + 1) * LANES]
            outs.append(yc * cos_ref[...]
                        + pltpu.roll(yc, LANES - 32, axis=1) * sina_ref[...]
                        + pltpu.roll(yc, 32, axis=1) * sinb_ref[...])
        return jnp.concatenate(outs, axis=1)

    rq = rotary(proj(OFF_R_Q, RET_QK_W)).astype(BF16)
    rk = (rotary(proj(OFF_R_K, RET_QK_W)) * (RET_QK_DIM ** -0.5)).astype(BF16)

    for half in halves:
        sl = slice(half * MXU_WIDTH, (half + 1) * MXU_WIDTH)
        q_out[:, sl] = (yq[half] * lax.rsqrt(_group_mean_sq(yq[half], bd) + EPS)
                        * (qg_ref[...] * (DA_QK_DIM ** -0.5 * LOG2E))).astype(BF16)
        k_out[:, sl] = (yk[half] * lax.rsqrt(_group_mean_sq(yk[half], bd) + EPS)
                        * kg_ref[...]).astype(BF16)

    rv = proj(OFF_R_V, RET_V_W).astype(BF16)
    ret_steps = _retention_steps(rq, rk, rv, proj(OFF_R_G, RET_V_W), rn_ref[...], bd, state_ref,
                                 ret_out)

    u = proj(OFF_SC_C, SC_WIDTH) * proj(OFF_SC_X, SC_WIDTH)
    ext_ref[SUBLANES:SUBLANES + tm, :] = u
    conv = (cb_ref[...]
            + cw_ref[0:1, :] * ext_ref[SUBLANES - 2:SUBLANES - 2 + tm, :]
            + cw_ref[1:2, :] * ext_ref[SUBLANES - 1:SUBLANES - 1 + tm, :]
            + cw_ref[2:3, :] * u)
    ext_ref[0:SUBLANES, :] = u[tm - SUBLANES:, :]
    sc_out[...] = (proj(OFF_SC_B, SC_WIDTH) * conv).astype(BF16)

    for c in range(N_BRANCH * D_MODEL // GATE_CHUNK):
        next(ret_steps, None)
        sl = slice(c * GATE_CHUNK, (c + 1) * GATE_CHUNK)
        z = proj(OFF_GATE + c * GATE_CHUNK, GATE_CHUNK) + bg_ref[:, sl]
        gate_out[:, sl] = _sigmoid(z).astype(BF16)
    for _ in ret_steps:
        pass


def _in_proj(x2d, seq, layer, g, w, bg, qg, kg, cos_t, sina_t, sinb_t, cw, cb, bd, rn):
    t = x2d.shape[0]
    tm = ROW_BLOCK
    blocks_per_seq = seq // tm
    row = lambda width: pl.BlockSpec((tm, width), lambda i: (i, 0))
    tab = pl.BlockSpec((tm, LANES), lambda i: (i % blocks_per_seq, 0))
    out_widths = (DA_QK_W, DA_QK_W, RET_V_W, SC_WIDTH, N_BRANCH * D_MODEL)
    out_specs = [row(wd) for wd in out_widths]
    out_shape = [jax.ShapeDtypeStruct((t, wd), BF16) for wd in out_widths]
    out_specs.insert(2, pl.BlockSpec((1, tm // ATT_BLOCK, DA_V_W, ATT_BLOCK),
                                     lambda i: (i // blocks_per_seq, i % blocks_per_seq, 0, 0)))
    out_shape.insert(2, jax.ShapeDtypeStruct((t // seq, seq // ATT_BLOCK, DA_V_W, ATT_BLOCK), BF16))
    return pl.pallas_call(
        functools.partial(_in_proj_kernel, blocks_per_seq=blocks_per_seq),
        grid=(t // tm,),
        in_specs=[row(D_MODEL), _resident(g.shape), _resident_layer(w.shape, layer),
                  _resident(bg.shape),
                  _resident(qg.shape), _resident(kg.shape), tab, tab, tab,
                  _resident(cw.shape), _resident(cb.shape), _resident(bd.shape),
                  _resident(rn.shape)],
        out_specs=out_specs,
        out_shape=out_shape,
        scratch_shapes=[pltpu.VMEM((tm, D_MODEL), BF16),
                        pltpu.VMEM((tm + SUBLANES, SC_WIDTH), F32),
                        pltpu.VMEM((RET_QK_W, RET_V_W), F32)],
        compiler_params=pltpu.CompilerParams(dimension_semantics=("arbitrary",),
                                             vmem_limit_bytes=VMEM_LIMIT),
        name="in_proj",
    )(x2d, g, w, bg, qg, kg, cos_t, sina_t, sinb_t, cw, cb, bd, rn)


def _diff_attn_kernel(q_ref, k_ref, vt_ref, nb_ref, lam_ref, sg_ref, o_ref,
                      s0_ref, s1_ref, acc0_ref, acc1_ref, *, lam_init):
    tq = o_ref.shape[1]
    qi = pl.program_id(2)
    n_q = pl.num_programs(2)

    def masked_q(block):
        q = q_ref[0, pl.ds(pl.multiple_of(block * tq, tq), tq), :]
        lane = lax.broadcasted_iota(jnp.int32, q.shape, 1)
        zero = jnp.zeros_like(q)
        return jnp.where(lane < DA_QK_DIM, q, zero), jnp.where(lane >= DA_QK_DIM, q, zero)

    q_map = masked_q(qi)
    s_refs = (s0_ref, s1_ref)
    acc_refs = (acc0_ref, acc1_ref)

    def put_scores(mp, j, far=False, q_block=qi, q_rows=None):
        start = pl.multiple_of(j * tq, tq)
        q_rows = q_map[mp] if q_rows is None else q_rows
        s = lax.dot_general(k_ref[0, pl.ds(start, tq), :], q_rows, (((1,), (1,)), ((), ())),
                            preferred_element_type=F32)
        s_refs[mp][...] = s if far else s + nb_ref[0, jnp.minimum(q_block - j, 2)]

    def update(mp, j, stats):
        m, l = stats
        m_new = jnp.maximum(m, jnp.max(s_refs[mp][...], axis=0, keepdims=True))
        alpha = jnp.exp2(m - m_new)
        p = jnp.exp2(s_refs[mp][...] - m_new)
        l = alpha * l + jnp.sum(p, axis=0, keepdims=True)
        acc_refs[mp][...] = alpha * acc_refs[mp][...] + jnp.dot(
            vt_ref[0, j], p.astype(BF16), preferred_element_type=F32)
        return m_new, l

    def step(j, stats, far=False):
        put_scores(1, j, far)
        stats0 = update(0, j, stats[0])
        put_scores(0, j + 1, far)
        stats1 = update(1, j, stats[1])
        return stats0, stats1

    init = (jnp.full((1, tq), NEG, F32), jnp.zeros((1, tq), F32))
    acc0_ref[...] = jnp.zeros_like(acc0_ref)
    acc1_ref[...] = jnp.zeros_like(acc1_ref)

    @pl.when(qi == 0)
    def _():
        put_scores(0, 0)

    n_far = jnp.maximum(qi - 2, 0)

    def far_pair(t, stats):
        return step(2 * t + 1, step(2 * t, stats, far=True), far=True)

    stats = lax.fori_loop(0, lax.shift_right_logical(n_far, 1), far_pair, (init, init))
    stats = lax.cond((n_far & 1) == 1, lambda st: step(n_far - 1, st, far=True), lambda st: st, stats)
    def finish(stats):
        put_scores(1, qi)
        _, l0 = update(0, qi, stats[0])
        nxt = jnp.minimum(qi + 1, n_q - 1)
        put_scores(0, 0, q_block=nxt, q_rows=masked_q(nxt)[0])
        _, l1 = update(1, qi, stats[1])

        lp = lam_ref[...]
        lam = (jnp.exp(jnp.sum(lp[0:1] * lp[1:2], axis=1, keepdims=True))
               - jnp.exp(jnp.sum(lp[2:3] * lp[3:4], axis=1, keepdims=True)) + lam_init)
        o = acc0_ref[...] * (1.0 / l0) - lam * (acc1_ref[...] * (1.0 / l1))
        ms = jnp.mean(o * o, axis=0, keepdims=True)
        o_ref[0] = (o * lax.rsqrt(ms + EPS) * (sg_ref[...] * (1.0 - lam_init))).T.astype(BF16)
        return 0

    lax.cond(
        qi >= 2, lambda st: finish(step(qi - 1, step(qi - 2, st))),
        lambda st: lax.cond(qi == 1, lambda s1: finish(step(0, s1)), finish, st), stats)


def _diff_attn(q, k, vt, nb, lam_p, sg, lam_init):
    b, s, _ = q.shape
    tq = ATT_BLOCK
    return pl.pallas_call(
        functools.partial(_diff_attn_kernel, lam_init=lam_init),
        grid=(b, DA_HEADS, s // tq),
        in_specs=[pl.BlockSpec((1, s, LANES), lambda bi, h, i: (bi, 0, h)),
                  pl.BlockSpec((1, s, LANES), lambda bi, h, i: (bi, 0, h)),
                  pl.BlockSpec((1, s // tq, DA_V_DIM, tq), lambda bi, h, i: (bi, 0, h, 0)),
                  pl.BlockSpec((1, 3, tq, tq), lambda bi, h, i: (h, 0, 0, 0)),
                  pl.BlockSpec(lam_p.shape, lambda bi, h, i: (0, 0)),
                  pl.BlockSpec(sg.shape, lambda bi, h, i: (0, 0))],
        out_specs=pl.BlockSpec((1, tq, LANES), lambda bi, h, i: (bi, i, h)),
        out_shape=jax.ShapeDtypeStruct((b, s, DA_V_W), BF16),
        scratch_shapes=[pltpu.VMEM((tq, tq), F32), pltpu.VMEM((tq, tq), F32),
                        pltpu.VMEM((DA_V_DIM, tq), F32), pltpu.VMEM((DA_V_DIM, tq), F32)],
        compiler_params=pltpu.CompilerParams(
            dimension_semantics=("arbitrary", "arbitrary", "arbitrary"),
            vmem_limit_bytes=VMEM_LIMIT),
        name="diff_attn",
    )(q, k, vt, nb, lam_p, sg)


def _ret_log_gamma(head):
    lg = [math.log(1.0 - 2.0 ** (-5.0 - h)) for h in range(RET_HEADS)]
    out = jnp.full(head.shape, lg[RET_HEADS - 1], F32)
    for h in range(RET_HEADS - 2, -1, -1):
        out = jnp.where(head == h, lg[h], out)
    return out


def _retention_steps(q, k, v, gate, ng, bd, state_ref, o_ref):
    c = RET_CHUNK
    w = RET_QK_W
    n_chunks = q.shape[0] // c
    head_shift = RET_QK_DIM.bit_length() - 1
    chunk_shift = c.bit_length() - 1
    assert 1 << head_shift == RET_QK_DIM == RET_V_DIM and 1 << chunk_shift == c
    lane_head = lax.broadcasted_iota(jnp.int32, (c, w), 1) >> head_shift
    row = lax.broadcasted_iota(jnp.int32, (c, w), 0).astype(F32)
    lg_lane = _ret_log_gamma(lane_head)
    q_decay = jnp.exp(lg_lane * (row + 1.0))
    k_decay = jnp.exp(lg_lane * (c - 1.0 - row))
    chunk_decay = jnp.exp(lg_lane[0:1, :] * float(c))
    srow = lax.broadcasted_iota(jnp.int32, (RET_HEADS * c, c), 0)
    scol = lax.broadcasted_iota(jnp.int32, (RET_HEADS * c, c), 1)
    diff = ((srow & (c - 1)) - scol).astype(F32)
    d_intra = jnp.where(diff >= 0,
                        jnp.exp(_ret_log_gamma(srow >> chunk_shift) * jnp.maximum(diff, 0.0)), 0.0)
    st_row_head = lax.broadcasted_iota(jnp.int32, (w, w), 0) >> head_shift
    st_col_head = lax.broadcasted_iota(jnp.int32, (w, w), 1) >> head_shift
    same_head = st_row_head == st_col_head

    for n in range(n_chunks):
        sl = slice(n * c, (n + 1) * c)
        qc, kc, vc = q[sl], k[sl], v[sl]
        zero = jnp.zeros_like(qc)
        qs = jnp.concatenate([jnp.where(lane_head == h, qc, zero) for h in range(RET_HEADS)], axis=0)
        s = lax.dot_general(qs, kc, (((1,), (1,)), ((), ())), preferred_element_type=F32)
        state = state_ref[...]
        cross = jnp.dot(qc, state.astype(BF16), preferred_element_type=F32) * q_decay
        k_dec = (kc.astype(F32) * k_decay).T.astype(BF16)
        kv = jnp.dot(k_dec, vc, preferred_element_type=F32)
        state_ref[...] = state * chunk_decay + jnp.where(same_head, kv, 0.0)
        yield
        o_all = jnp.dot((s * d_intra).astype(BF16), vc, preferred_element_type=F32)
        inner = jnp.zeros((c, w), F32)
        for h in range(RET_HEADS):
            inner = jnp.where(lane_head == h, o_all[h * c:(h + 1) * c], inner)
        o = inner + cross

        y = o * lax.rsqrt(_group_mean_sq(o, bd) + EPS) * ng
        g = gate[sl]
        o_ref[sl, :] = (y * (g * _sigmoid(g))).astype(BF16)


def _merge_ffn_kernel(x_ref, oda_ref, oret_ref, osc_ref, gate_ref, wda_ref, wret_ref, wsc_ref,
                      wout_ref, ng_ref, win_ref, cw_ref, cb_ref, wo_ref,
                      out_ref, h_ref, ext_ref, carry_ref, act_ref, *, blocks_per_seq):
    tm = x_ref.shape[0]
    d = D_MODEL
    y = (gate_ref[:, 0:d].astype(F32)
         * jnp.dot(oda_ref[...], wda_ref[...], preferred_element_type=F32))
    y += (gate_ref[:, d:2 * d].astype(F32)
          * jnp.dot(oret_ref[...], wret_ref[...], preferred_element_type=F32))
    y += (gate_ref[:, 2 * d:3 * d].astype(F32)
          * jnp.dot(osc_ref[...], wsc_ref[...], preferred_element_type=F32))
    x1 = x_ref[...] + jnp.dot(y.astype(BF16), wout_ref[...], preferred_element_type=F32)
    out_ref[...] = x1
    ms = jnp.mean(x1 * x1, axis=-1, keepdims=True)
    h_ref[...] = (x1 * lax.rsqrt(ms + EPS) * ng_ref[...]).astype(BF16)

    @pl.when(pl.program_id(0) % blocks_per_seq == 0)
    def _():
        carry_ref[...] = jnp.zeros_like(carry_ref)

    for c in range(N_FF_CHUNKS):
        sl = slice(c * FF_CHUNK, (c + 1) * FF_CHUNK)
        g = jnp.dot(h_ref[...], win_ref[:, sl], preferred_element_type=F32)
        u = jnp.dot(h_ref[...], win_ref[:, D_FF + c * FF_CHUNK:D_FF + (c + 1) * FF_CHUNK],
                    preferred_element_type=F32)
        ext_ref[0:SUBLANES, :] = carry_ref[:, sl]
        ext_ref[SUBLANES:SUBLANES + tm, :] = g
        gc = (cb_ref[:, sl]
              + cw_ref[0:1, sl] * ext_ref[SUBLANES - 2:SUBLANES - 2 + tm, :]
              + cw_ref[1:2, sl] * ext_ref[SUBLANES - 1:SUBLANES - 1 + tm, :]
              + cw_ref[2:3, sl] * g)
        carry_ref[:, sl] = g[tm - SUBLANES:, :]
        act_ref[:, sl] = (gc * _sigmoid(gc) * u).astype(BF16)

    out_ref[...] += jnp.dot(act_ref[...], wo_ref[...], preferred_element_type=F32)


def _merge_ffn(x2d, seq, layer, oda, oret, osc, gates, wda, wret, wsc, wout, ng, win, cw, cb, wo):
    t = x2d.shape[0]
    tm = ROW_BLOCK
    blocks_per_seq = seq // tm
    row = lambda width: pl.BlockSpec((tm, width), lambda i: (i, 0))
    return pl.pallas_call(
        functools.partial(_merge_ffn_kernel, blocks_per_seq=blocks_per_seq),
        grid=(t // tm,),
        in_specs=[row(D_MODEL), row(DA_V_W), row(RET_V_W), row(SC_WIDTH), row(N_BRANCH * D_MODEL)]
                 + [_resident_layer(a.shape, layer) if a.ndim == 3 else _resident(a.shape)
                    for a in (wda, wret, wsc, wout, ng, win, cw, cb, wo)],
        out_specs=row(D_MODEL),
        out_shape=jax.ShapeDtypeStruct((t, D_MODEL), F32),
        scratch_shapes=[pltpu.VMEM((tm, D_MODEL), BF16),
                        pltpu.VMEM((tm + SUBLANES, FF_CHUNK), F32),
                        pltpu.VMEM((SUBLANES, D_FF), F32),
                        pltpu.VMEM((tm, D_FF), BF16)],
        compiler_params=pltpu.CompilerParams(dimension_semantics=("arbitrary",),
                                             vmem_limit_bytes=VMEM_LIMIT),
        name="merge_ffn",
    )(x2d, oda, oret, osc, gates, wda, wret, wsc, wout, ng, win, cw, cb, wo)


def _rel_bucket(n):
    max_exact = REL_BUCKETS // 2
    nf = jnp.maximum(n, 1).astype(F32)
    large = max_exact + (jnp.log(nf / max_exact) / math.log(REL_MAX_DIST / max_exact)
                         * (REL_BUCKETS - max_exact)).astype(jnp.int32)
    large = jnp.minimum(large, REL_BUCKETS - 1)
    return jnp.where(n < max_exact, n, large)


def _near_bias_tiles(rel_bias, tq):
    hb = tq // 2
    assert tq == 2 * hb and hb >= REL_MAX_DIST
    n_heads = rel_bias.shape[1]
    dist = jnp.arange(2 * hb)
    shifted = (rel_bias[_rel_bucket(dist)] - rel_bias[REL_BUCKETS - 1]) * LOG2E
    by_offset = jnp.concatenate([jnp.full((hb - 1, n_heads), NEG, F32), shifted,
                                 jnp.zeros((1, n_heads), F32)], axis=0).T
    m = by_offset.shape[1]
    skew = jnp.tile(by_offset, (1, hb))[:, :hb * (m - 1)].reshape(n_heads, hb, m - 1)
    near = skew[:, :, hb - 1:2 * hb - 1]
    next_ = skew[:, :, 2 * hb - 1:3 * hb - 1]
    zero = jnp.zeros_like(near)
    masked = jnp.full_like(near, NEG)
    grid2 = lambda rows: jnp.concatenate([jnp.concatenate(r, axis=-1) for r in rows], axis=-2)
    diag = grid2([[near, next_], [masked, near]])
    prev = grid2([[zero, zero], [next_, zero]])
    return jnp.stack([diag, prev, jnp.zeros_like(diag)], axis=1).astype(F32)


def _rotary_tables(seq):
    half = RET_QK_DIM // 2
    inv = ROPE_THETA ** (-jnp.arange(half, dtype=F32) / half)
    ang = jnp.arange(seq).astype(F32)[:, None] * inv[None, :]
    cos, sin = jnp.cos(ang), jnp.sin(ang)
    zero = jnp.zeros_like(sin)
    reps = LANES // RET_QK_DIM
    cos_t = jnp.tile(jnp.concatenate([cos, cos], axis=1), (1, reps))
    sina_t = jnp.tile(jnp.concatenate([-sin, zero], axis=1), (1, reps))
    sinb_t = jnp.tile(jnp.concatenate([zero, sin], axis=1), (1, reps))
    return cos_t, sina_t, sinb_t


@jax.jit
def _forward(x, rel_bias, norm_mix_g, w_in, b_gate, da_q_norm_g, da_k_norm_g, da_lambda,
             da_subln_g, ret_norm_g, sc_conv_w, sc_conv_b, w_branch_da, w_branch_ret,
             w_branch_sc, w_out, norm_ffn_g, w_ffn_in, ffn_conv_w, ffn_conv_b, w_ffn_out):
    b, s, d = x.shape
    depth = w_in.shape[0]
    assert d == D_MODEL and s % ROW_BLOCK == 0 and s % ATT_BLOCK == 0
    nb = _near_bias_tiles(rel_bias.astype(F32), ATT_BLOCK)
    cos_t, sina_t, sinb_t = _rotary_tables(s)
    assert NORM_GROUP == DA_QK_DIM == RET_V_DIM and RET_V_W == MXU_WIDTH
    group = jnp.arange(MXU_WIDTH) // NORM_GROUP
    bd = (group[:, None] == group[None, :]).astype(BF16)

    w_in, w_branch_da, w_branch_ret, w_branch_sc, w_out, w_ffn_in, w_ffn_out = (
        w.astype(BF16) for w in (w_in, w_branch_da, w_branch_ret, w_branch_sc, w_out, w_ffn_in,
                                 w_ffn_out))
    x2d = x.reshape(b * s, d)
    for l in range(depth):
        lam_init = 0.8 - 0.6 * math.exp(-0.3 * l)
        q, k, vt, oret, osc, gates = _in_proj(
            x2d, s, l, norm_mix_g[l][None, :], w_in, b_gate[l][None, :],
            jnp.tile(da_q_norm_g[l], MXU_WIDTH // DA_QK_DIM)[None, :],
            jnp.tile(da_k_norm_g[l], MXU_WIDTH // DA_QK_DIM)[None, :],
            cos_t, sina_t, sinb_t, sc_conv_w[l], sc_conv_b[l][None, :], bd,
            jnp.tile(ret_norm_g[l], RET_HEADS)[None, :])
        to3 = lambda a: a.reshape(b, s, a.shape[-1])
        oda = _diff_attn(to3(q), to3(k), vt, nb, da_lambda[l].astype(F32),
                         da_subln_g[l][:, None], lam_init)
        x2d = _merge_ffn(
            x2d, s, l, oda.reshape(b * s, DA_V_W), oret, osc, gates,
            w_branch_da, w_branch_ret, w_branch_sc, w_out, norm_ffn_g[l][None, :],
            w_ffn_in, ffn_conv_w[l], ffn_conv_b[l][None, :], w_ffn_out)
    return x2d.reshape(b, s, d)


def kernel(x, rel_bias, norm_mix_g, w_in, b_gate, da_q_norm_g, da_k_norm_g, da_lambda, da_subln_g,
           ret_norm_g, sc_conv_w, sc_conv_b, w_branch_da, w_branch_ret, w_branch_sc, w_out,
           norm_ffn_g, w_ffn_in, ffn_conv_w, ffn_conv_b, w_ffn_out):
    return _forward(x, rel_bias, norm_mix_g, w_in, b_gate, da_q_norm_g, da_k_norm_g, da_lambda,
                    da_subln_g, ret_norm_g, sc_conv_w, sc_conv_b, w_branch_da, w_branch_ret,
                    w_branch_sc, w_out, norm_ffn_g, w_ffn_in, ffn_conv_w, ffn_conv_b, w_ffn_out)
```

```python
import functools
import math

import jax
import jax.numpy as jnp
from jax import lax
from jax.experimental import pallas as pl
from jax.experimental.pallas import tpu as pltpu

D_MODEL = 1024
DA_HEADS = 4
DA_QK_DIM = 64
DA_V_DIM = 2 * DA_QK_DIM
RET_HEADS = 4
RET_QK_DIM = 64
RET_V_DIM = 64
SC_WIDTH = 256
CONV_W = 3
D_FF = 2816
REL_BUCKETS = 32
REL_MAX_DIST = 128
ROPE_THETA = 10000.0
EPS = 1e-6
N_BRANCH = 3

DA_QK_W = DA_HEADS * 2 * DA_QK_DIM
DA_V_W = DA_HEADS * DA_V_DIM
RET_QK_W = RET_HEADS * RET_QK_DIM
RET_V_W = RET_HEADS * RET_V_DIM

OFF_DA_Q = 0
OFF_DA_K = OFF_DA_Q + DA_QK_W
OFF_DA_V = OFF_DA_K + DA_QK_W
OFF_R_Q = OFF_DA_V + DA_V_W
OFF_R_K = OFF_R_Q + RET_QK_W
OFF_R_V = OFF_R_K + RET_QK_W
OFF_R_G = OFF_R_V + RET_V_W
OFF_SC_B = OFF_R_G + RET_V_W
OFF_SC_C = OFF_SC_B + SC_WIDTH
OFF_SC_X = OFF_SC_C + SC_WIDTH
OFF_GATE = OFF_SC_X + SC_WIDTH
IN_WIDTH = OFF_GATE + N_BRANCH * D_MODEL

LANES = 128
MXU_WIDTH = 256
NORM_GROUP = 64
SUBLANES = 8
ROW_BLOCK = 512
ATT_BLOCK = 512
RET_CHUNK = 128
GATE_CHUNK = 512
FF_CHUNK = 256
N_FF_CHUNKS = D_FF // FF_CHUNK
VMEM_LIMIT = 56 * 1024 * 1024
NEG = -1e30
LOG2E = math.log2(math.e)
BF16 = jnp.bfloat16
F32 = jnp.float32


def _resident(shape):
    nd = len(shape)
    return pl.BlockSpec(shape, lambda *_: (0,) * nd, pipeline_mode=pl.Buffered(1))


def _resident_layer(stacked_shape, layer):
    _, rows, cols = stacked_shape
    return pl.BlockSpec((None, rows, cols), lambda *_: (layer, 0, 0), pipeline_mode=pl.Buffered(1))


def _group_mean_sq(y, bd):
    return jnp.dot((y * y).astype(BF16), bd, preferred_element_type=F32) * (1.0 / NORM_GROUP)


def _sigmoid(z):
    return 1.0 / (1.0 + jnp.exp(-z))


def _in_proj_kernel(x_ref, g_ref, w_ref, bg_ref, qg_ref, kg_ref, cos_ref, sina_ref, sinb_ref,
                    cw_ref, cb_ref, bd_ref, rn_ref,
                    q_out, k_out, vt_out, ret_out, sc_out, gate_out,
                    h_ref, ext_ref, state_ref, *, blocks_per_seq):
    tm = x_ref.shape[0]

    @pl.when(pl.program_id(0) % blocks_per_seq == 0)
    def _():
        ext_ref[0:SUBLANES, :] = jnp.zeros((SUBLANES, SC_WIDTH), F32)
        state_ref[...] = jnp.zeros_like(state_ref)

    x = x_ref[...]
    ms = jnp.mean(x * x, axis=-1, keepdims=True)
    h_ref[...] = (x * lax.rsqrt(ms + EPS) * g_ref[...]).astype(BF16)
    bd = bd_ref[...]

    def proj(off, width):
        return jnp.dot(h_ref[...], w_ref[:, off:off + width], preferred_element_type=F32)

    halves = range(DA_QK_W // MXU_WIDTH)
    yq = [proj(OFF_DA_Q + half * MXU_WIDTH, MXU_WIDTH) for half in halves]
    yk = [proj(OFF_DA_K + half * MXU_WIDTH, MXU_WIDTH) for half in halves]
    v = proj(OFF_DA_V, DA_V_W)
    for c in range(tm // ATT_BLOCK):
        vt_out[0, c] = v[c * ATT_BLOCK:(c + 1) * ATT_BLOCK, :].T.astype(BF16)

    def rotary(y):
        outs = []
        for c in range(RET_QK_W // LANES):
            yc = y[:, c * LANES:(c + 1) * LANES]
            outs.append(yc * cos_ref[...]
                        + pltpu.roll(yc, LANES - 32, axis=1) * sina_ref[...]
                        + pltpu.roll(yc, 32, axis=1) * sinb_ref[...])
        return jnp.concatenate(outs, axis=1)

    rq = rotary(proj(OFF_R_Q, RET_QK_W)).astype(BF16)
    rk = (rotary(proj(OFF_R_K, RET_QK_W)) * (RET_QK_DIM ** -0.5)).astype(BF16)

    for half in halves:
        sl = slice(half * MXU_WIDTH, (half + 1) * MXU_WIDTH)
        q_out[:, sl] = (yq[half] * lax.rsqrt(_group_mean_sq(yq[half], bd) + EPS)
                        * (qg_ref[...] * (DA_QK_DIM ** -0.5 * LOG2E))).astype(BF16)
        k_out[:, sl] = (yk[half] * lax.rsqrt(_group_mean_sq(yk[half], bd) + EPS)
                        * kg_ref[...]).astype(BF16)

    rv = proj(OFF_R_V, RET_V_W).astype(BF16)
    ret_steps = _retention_steps(rq, rk, rv, proj(OFF_R_G, RET_V_W), rn_ref[...], bd, state_ref,
                                 ret_out)

    next(ret_steps, None)
    u = proj(OFF_SC_C, SC_WIDTH) * proj(OFF_SC_X, SC_WIDTH)
    ext_ref[SUBLANES:SUBLANES + tm, :] = u
    conv = (cb_ref[...]
            + cw_ref[0:1, :] * ext_ref[SUBLANES - 2:SUBLANES - 2 + tm, :]
            + cw_ref[1:2, :] * ext_ref[SUBLANES - 1:SUBLANES - 1 + tm, :]
            + cw_ref[2:3, :] * u)
    ext_ref[0:SUBLANES, :] = u[tm - SUBLANES:, :]
    next(ret_steps, None)
    sc_out[...] = (proj(OFF_SC_B, SC_WIDTH) * conv).astype(BF16)

    for c in range(N_BRANCH * D_MODEL // GATE_CHUNK):
        next(ret_steps, None)
        sl = slice(c * GATE_CHUNK, (c + 1) * GATE_CHUNK)
        z = proj(OFF_GATE + c * GATE_CHUNK, GATE_CHUNK) + bg_ref[:, sl]
        gate_out[:, sl] = _sigmoid(z).astype(BF16)
    for _ in ret_steps:
        pass


def _in_proj(x2d, seq, layer, g, w, bg, qg, kg, cos_t, sina_t, sinb_t, cw, cb, bd, rn):
    t = x2d.shape[0]
    tm = ROW_BLOCK
    blocks_per_seq = seq // tm
    row = lambda width: pl.BlockSpec((tm, width), lambda i: (i, 0))
    tab = pl.BlockSpec((tm, LANES), lambda i: (i % blocks_per_seq, 0))
    out_widths = (DA_QK_W, DA_QK_W, RET_V_W, SC_WIDTH, N_BRANCH * D_MODEL)
    out_specs = [row(wd) for wd in out_widths]
    out_shape = [jax.ShapeDtypeStruct((t, wd), BF16) for wd in out_widths]
    out_specs.insert(2, pl.BlockSpec((1, tm // ATT_BLOCK, DA_V_W, ATT_BLOCK),
                                     lambda i: (i // blocks_per_seq, i % blocks_per_seq, 0, 0)))
    out_shape.insert(2, jax.ShapeDtypeStruct((t // seq, seq // ATT_BLOCK, DA_V_W, ATT_BLOCK), BF16))
    return pl.pallas_call(
        functools.partial(_in_proj_kernel, blocks_per_seq=blocks_per_seq),
        grid=(t // tm,),
        in_specs=[row(D_MODEL), _resident(g.shape), _resident_layer(w.shape, layer),
                  _resident(bg.shape),
                  _resident(qg.shape), _resident(kg.shape), tab, tab, tab,
                  _resident(cw.shape), _resident(cb.shape), _resident(bd.shape),
                  _resident(rn.shape)],
        out_specs=out_specs,
        out_shape=out_shape,
        scratch_shapes=[pltpu.VMEM((tm, D_MODEL), BF16),
                        pltpu.VMEM((tm + SUBLANES, SC_WIDTH), F32),
                        pltpu.VMEM((RET_QK_W, RET_V_W), F32)],
        compiler_params=pltpu.CompilerParams(dimension_semantics=("arbitrary",),
                                             vmem_limit_bytes=VMEM_LIMIT),
        name="in_proj",
    )(x2d, g, w, bg, qg, kg, cos_t, sina_t, sinb_t, cw, cb, bd, rn)


def _diff_attn_kernel(q_ref, k_ref, vt_ref, nb_ref, lam_ref, sg_ref, o_ref,
                      s0_ref, s1_ref, acc0_ref, acc1_ref, *, lam_init):
    tq = o_ref.shape[1]
    qi = pl.program_id(2)
    n_q = pl.num_programs(2)

    def masked_q(block):
        q = q_ref[0, pl.ds(pl.multiple_of(block * tq, tq), tq), :]
        lane = lax.broadcasted_iota(jnp.int32, q.shape, 1)
        zero = jnp.zeros_like(q)
        return jnp.where(lane < DA_QK_DIM, q, zero), jnp.where(lane >= DA_QK_DIM, q, zero)

    q_map = masked_q(qi)
    s_refs = (s0_ref, s1_ref)
    acc_refs = (acc0_ref, acc1_ref)

    def put_scores(mp, j, far=False, q_block=qi, q_rows=None):
        start = pl.multiple_of(j * tq, tq)
        q_rows = q_map[mp] if q_rows is None else q_rows
        s = lax.dot_general(k_ref[0, pl.ds(start, tq), :], q_rows, (((1,), (1,)), ((), ())),
                            preferred_element_type=F32)
        s_refs[mp][...] = s if far else s + nb_ref[0, jnp.minimum(q_block - j, 2)]

    def update(mp, j, stats):
        m, l = stats
        m_new = jnp.maximum(m, jnp.max(s_refs[mp][...], axis=0, keepdims=True))
        alpha = jnp.exp2(m - m_new)
        p = jnp.exp2(s_refs[mp][...] - m_new)
        l = alpha * l + jnp.sum(p, axis=0, keepdims=True)
        acc_refs[mp][...] = alpha * acc_refs[mp][...] + jnp.dot(
            vt_ref[0, j], p.astype(BF16), preferred_element_type=F32)
        return m_new, l

    def step(j, stats, far=False):
        put_scores(1, j, far)
        stats0 = update(0, j, stats[0])
        put_scores(0, j + 1, far)
        stats1 = update(1, j, stats[1])
        return stats0, stats1

    init = (jnp.full((1, tq), NEG, F32), jnp.zeros((1, tq), F32))
    acc0_ref[...] = jnp.zeros_like(acc0_ref)
    acc1_ref[...] = jnp.zeros_like(acc1_ref)

    @pl.when(qi == 0)
    def _():
        put_scores(0, 0)

    n_far = jnp.maximum(qi - 2, 0)

    def far_pair(t, stats):
        return step(2 * t + 1, step(2 * t, stats, far=True), far=True)

    stats = lax.fori_loop(0, lax.shift_right_logical(n_far, 1), far_pair, (init, init))
    stats = lax.cond((n_far & 1) == 1, lambda st: step(n_far - 1, st, far=True), lambda st: st, stats)
    def finish(stats):
        put_scores(1, qi)
        _, l0 = update(0, qi, stats[0])
        nxt = jnp.minimum(qi + 1, n_q - 1)
        put_scores(0, 0, q_block=nxt, q_rows=masked_q(nxt)[0])
        _, l1 = update(1, qi, stats[1])

        lp = lam_ref[...]
        lam = (jnp.exp(jnp.sum(lp[0:1] * lp[1:2], axis=1, keepdims=True))
               - jnp.exp(jnp.sum(lp[2:3] * lp[3:4], axis=1, keepdims=True)) + lam_init)
        o = acc0_ref[...] * (1.0 / l0) - lam * (acc1_ref[...] * (1.0 / l1))
        ms = jnp.mean(o * o, axis=0, keepdims=True)
        o_ref[0] = (o * lax.rsqrt(ms + EPS) * (sg_ref[...] * (1.0 - lam_init))).T.astype(BF16)
        return 0

    lax.cond(
        qi >= 2, lambda st: finish(step(qi - 1, step(qi - 2, st))),
        lambda st: lax.cond(qi == 1, lambda s1: finish(step(0, s1)), finish, st), stats)


def _diff_attn(q, k, vt, nb, lam_p, sg, lam_init):
    b, s, _ = q.shape
    tq = ATT_BLOCK
    return pl.pallas_call(
        functools.partial(_diff_attn_kernel, lam_init=lam_init),
        grid=(b, DA_HEADS, s // tq),
        in_specs=[pl.BlockSpec((1, s, LANES), lambda bi, h, i: (bi, 0, h)),
                  pl.BlockSpec((1, s, LANES), lambda bi, h, i: (bi, 0, h)),
                  pl.BlockSpec((1, s // tq, DA_V_DIM, tq), lambda bi, h, i: (bi, 0, h, 0)),
                  pl.BlockSpec((1, 3, tq, tq), lambda bi, h, i: (h, 0, 0, 0)),
                  pl.BlockSpec(lam_p.shape, lambda bi, h, i: (0, 0)),
                  pl.BlockSpec(sg.shape, lambda bi, h, i: (0, 0))],
        out_specs=pl.BlockSpec((1, tq, LANES), lambda bi, h, i: (bi, i, h)),
        out_shape=jax.ShapeDtypeStruct((b, s, DA_V_W), BF16),
        scratch_shapes=[pltpu.VMEM((tq, tq), F32), pltpu.VMEM((tq, tq), F32),
                        pltpu.VMEM((DA_V_DIM, tq), F32), pltpu.VMEM((DA_V_DIM, tq), F32)],
        compiler_params=pltpu.CompilerParams(
            dimension_semantics=("arbitrary", "arbitrary", "arbitrary"),
            vmem_limit_bytes=VMEM_LIMIT),
        name="diff_attn",
    )(q, k, vt, nb, lam_p, sg)


def _ret_log_gamma(head):
    lg = [math.log(1.0 - 2.0 ** (-5.0 - h)) for h in range(RET_HEADS)]
    out = jnp.full(head.shape, lg[RET_HEADS - 1], F32)
    for h in range(RET_HEADS - 2, -1, -1):
        out = jnp.where(head == h, lg[h], out)
    return out


def _retention_steps(q, k, v, gate, ng, bd, state_ref, o_ref):
    c = RET_CHUNK
    w = RET_QK_W
    n_chunks = q.shape[0] // c
    head_shift = RET_QK_DIM.bit_length() - 1
    chunk_shift = c.bit_length() - 1
    assert 1 << head_shift == RET_QK_DIM == RET_V_DIM and 1 << chunk_shift == c
    lane_head = lax.broadcasted_iota(jnp.int32, (c, w), 1) >> head_shift
    row = lax.broadcasted_iota(jnp.int32, (c, w), 0).astype(F32)
    lg_lane = _ret_log_gamma(lane_head)
    q_decay = jnp.exp(lg_lane * (row + 1.0))
    k_decay = jnp.exp(lg_lane * (c - 1.0 - row))
    chunk_decay = jnp.exp(lg_lane[0:1, :] * float(c))
    srow = lax.broadcasted_iota(jnp.int32, (RET_HEADS * c, c), 0)
    scol = lax.broadcasted_iota(jnp.int32, (RET_HEADS * c, c), 1)
    diff = ((srow & (c - 1)) - scol).astype(F32)
    d_intra = jnp.where(diff >= 0,
                        jnp.exp(_ret_log_gamma(srow >> chunk_shift) * jnp.maximum(diff, 0.0)), 0.0)
    st_row_head = lax.broadcasted_iota(jnp.int32, (w, w), 0) >> head_shift
    st_col_head = lax.broadcasted_iota(jnp.int32, (w, w), 1) >> head_shift
    same_head = st_row_head == st_col_head

    for n in range(n_chunks):
        sl = slice(n * c, (n + 1) * c)
        qc, kc, vc = q[sl], k[sl], v[sl]
        zero = jnp.zeros_like(qc)
        qs = jnp.concatenate([jnp.where(lane_head == h, qc, zero) for h in range(RET_HEADS)], axis=0)
        s = lax.dot_general(qs, kc, (((1,), (1,)), ((), ())), preferred_element_type=F32)
        state = state_ref[...]
        cross = jnp.dot(qc, state.astype(BF16), preferred_element_type=F32) * q_decay
        k_dec = (kc.astype(F32) * k_decay).T.astype(BF16)
        kv = jnp.dot(k_dec, vc, preferred_element_type=F32)
        state_ref[...] = state * chunk_decay + jnp.where(same_head, kv, 0.0)
        yield
        o_all = jnp.dot((s * d_intra).astype(BF16), vc, preferred_element_type=F32)
        inner = jnp.zeros((c, w), F32)
        for h in range(RET_HEADS):
            inner = jnp.where(lane_head == h, o_all[h * c:(h + 1) * c], inner)
        o = inner + cross
        yield
        y = o * lax.rsqrt(_group_mean_sq(o, bd) + EPS) * ng
        g = gate[sl]
        o_ref[sl, :] = (y * (g * _sigmoid(g))).astype(BF16)


def _merge_ffn_kernel(x_ref, oda_ref, oret_ref, osc_ref, gate_ref, wda_ref, wret_ref, wsc_ref,
                      wout_ref, ng_ref, win_ref, cw_ref, cb_ref, wo_ref,
                      out_ref, h_ref, ext_ref, carry_ref, act_ref, *, blocks_per_seq):
    tm = x_ref.shape[0]
    d = D_MODEL
    y = (gate_ref[:, 0:d].astype(F32)
         * jnp.dot(oda_ref[...], wda_ref[...], preferred_element_type=F32))
    y += (gate_ref[:, d:2 * d].astype(F32)
          * jnp.dot(oret_ref[...], wret_ref[...], preferred_element_type=F32))
    y += (gate_ref[:, 2 * d:3 * d].astype(F32)
          * jnp.dot(osc_ref[...], wsc_ref[...], preferred_element_type=F32))
    x1 = x_ref[...] + jnp.dot(y.astype(BF16), wout_ref[...], preferred_element_type=F32)
    out_ref[...] = x1
    ms = jnp.mean(x1 * x1, axis=-1, keepdims=True)
    h_ref[...] = (x1 * lax.rsqrt(ms + EPS) * ng_ref[...]).astype(BF16)

    @pl.when(pl.program_id(0) % blocks_per_seq == 0)
    def _():
        carry_ref[...] = jnp.zeros_like(carry_ref)

    for c in range(N_FF_CHUNKS):
        sl = slice(c * FF_CHUNK, (c + 1) * FF_CHUNK)
        g = jnp.dot(h_ref[...], win_ref[:, sl], preferred_element_type=F32)
        u = jnp.dot(h_ref[...], win_ref[:, D_FF + c * FF_CHUNK:D_FF + (c + 1) * FF_CHUNK],
                    preferred_element_type=F32)
        ext_ref[0:SUBLANES, :] = carry_ref[:, sl]
        ext_ref[SUBLANES:SUBLANES + tm, :] = g
        gc = (cb_ref[:, sl]
              + cw_ref[0:1, sl] * ext_ref[SUBLANES - 2:SUBLANES - 2 + tm, :]
              + cw_ref[1:2, sl] * ext_ref[SUBLANES - 1:SUBLANES - 1 + tm, :]
              + cw_ref[2:3, sl] * g)
        carry_ref[:, sl] = g[tm - SUBLANES:, :]
        act_ref[:, sl] = (gc * _sigmoid(gc) * u).astype(BF16)

    out_ref[...] += jnp.dot(act_ref[...], wo_ref[...], preferred_element_type=F32)


def _merge_ffn(x2d, seq, layer, oda, oret, osc, gates, wda, wret, wsc, wout, ng, win, cw, cb, wo):
    t = x2d.shape[0]
    tm = ROW_BLOCK
    blocks_per_seq = seq // tm
    row = lambda width: pl.BlockSpec((tm, width), lambda i: (i, 0))
    return pl.pallas_call(
        functools.partial(_merge_ffn_kernel, blocks_per_seq=blocks_per_seq),
        grid=(t // tm,),
        in_specs=[row(D_MODEL), row(DA_V_W), row(RET_V_W), row(SC_WIDTH), row(N_BRANCH * D_MODEL)]
                 + [_resident_layer(a.shape, layer) if a.ndim == 3 else _resident(a.shape)
                    for a in (wda, wret, wsc, wout, ng, win, cw, cb, wo)],
        out_specs=row(D_MODEL),
        out_shape=jax.ShapeDtypeStruct((t, D_MODEL), F32),
        scratch_shapes=[pltpu.VMEM((tm, D_MODEL), BF16),
                        pltpu.VMEM((tm + SUBLANES, FF_CHUNK), F32),
                        pltpu.VMEM((SUBLANES, D_FF), F32),
                        pltpu.VMEM((tm, D_FF), BF16)],
        compiler_params=pltpu.CompilerParams(dimension_semantics=("arbitrary",),
                                             vmem_limit_bytes=VMEM_LIMIT),
        name="merge_ffn",
    )(x2d, oda, oret, osc, gates, wda, wret, wsc, wout, ng, win, cw, cb, wo)


def _rel_bucket(n):
    max_exact = REL_BUCKETS // 2
    nf = jnp.maximum(n, 1).astype(F32)
    large = max_exact + (jnp.log(nf / max_exact) / math.log(REL_MAX_DIST / max_exact)
                         * (REL_BUCKETS - max_exact)).astype(jnp.int32)
    large = jnp.minimum(large, REL_BUCKETS - 1)
    return jnp.where(n < max_exact, n, large)


def _near_bias_tiles(rel_bias, tq):
    hb = tq // 2
    assert tq == 2 * hb and hb >= REL_MAX_DIST
    n_heads = rel_bias.shape[1]
    dist = jnp.arange(2 * hb)
    shifted = (rel_bias[_rel_bucket(dist)] - rel_bias[REL_BUCKETS - 1]) * LOG2E
    by_offset = jnp.concatenate([jnp.full((hb - 1, n_heads), NEG, F32), shifted,
                                 jnp.zeros((1, n_heads), F32)], axis=0).T
    m = by_offset.shape[1]
    skew = jnp.tile(by_offset, (1, hb))[:, :hb * (m - 1)].reshape(n_heads, hb, m - 1)
    near = skew[:, :, hb - 1:2 * hb - 1]
    next_ = skew[:, :, 2 * hb - 1:3 * hb - 1]
    zero = jnp.zeros_like(near)
    masked = jnp.full_like(near, NEG)
    grid2 = lambda rows: jnp.concatenate([jnp.concatenate(r, axis=-1) for r in rows], axis=-2)
    diag = grid2([[near, next_], [masked, near]])
    prev = grid2([[zero, zero], [next_, zero]])
    return jnp.stack([diag, prev, jnp.zeros_like(diag)], axis=1).astype(F32)


def _rotary_tables(seq):
    half = RET_QK_DIM // 2
    inv = ROPE_THETA ** (-jnp.arange(half, dtype=F32) / half)
    ang = jnp.arange(seq).astype(F32)[:, None] * inv[None, :]
    cos, sin = jnp.cos(ang), jnp.sin(ang)
    zero = jnp.zeros_like(sin)
    reps = LANES // RET_QK_DIM
    cos_t = jnp.tile(jnp.concatenate([cos, cos], axis=1), (1, reps))
    sina_t = jnp.tile(jnp.concatenate([-sin, zero], axis=1), (1, reps))
    sinb_t = jnp.tile(jnp.concatenate([zero, sin], axis=1), (1, reps))
    return cos_t, sina_t, sinb_t


@jax.jit
def _forward(x, rel_bias, norm_mix_g, w_in, b_gate, da_q_norm_g, da_k_norm_g, da_lambda,
             da_subln_g, ret_norm_g, sc_conv_w, sc_conv_b, w_branch_da, w_branch_ret,
             w_branch_sc, w_out, norm_ffn_g, w_ffn_in, ffn_conv_w, ffn_conv_b, w_ffn_out):
    b, s, d = x.shape
    depth = w_in.shape[0]
    assert d == D_MODEL and s % ROW_BLOCK == 0 and s % ATT_BLOCK == 0
    nb = _near_bias_tiles(rel_bias.astype(F32), ATT_BLOCK)
    cos_t, sina_t, sinb_t = _rotary_tables(s)
    assert NORM_GROUP == DA_QK_DIM == RET_V_DIM and RET_V_W == MXU_WIDTH
    group = jnp.arange(MXU_WIDTH) // NORM_GROUP
    bd = (group[:, None] == group[None, :]).astype(BF16)

    w_in, w_branch_da, w_branch_ret, w_branch_sc, w_out, w_ffn_in, w_ffn_out = (
        w.astype(BF16) for w in (w_in, w_branch_da, w_branch_ret, w_branch_sc, w_out, w_ffn_in,
                                 w_ffn_out))
    x2d = x.reshape(b * s, d)
    for l in range(depth):
        lam_init = 0.8 - 0.6 * math.exp(-0.3 * l)
        q, k, vt, oret, osc, gates = _in_proj(
            x2d, s, l, norm_mix_g[l][None, :], w_in, b_gate[l][None, :],
            jnp.tile(da_q_norm_g[l], MXU_WIDTH // DA_QK_DIM)[None, :],
            jnp.tile(da_k_norm_g[l], MXU_WIDTH // DA_QK_DIM)[None, :],
            cos_t, sina_t, sinb_t, sc_conv_w[l], sc_conv_b[l][None, :], bd,
            jnp.tile(ret_norm_g[l], RET_HEADS)[None, :])
        to3 = lambda a: a.reshape(b, s, a.shape[-1])
        oda = _diff_attn(to3(q), to3(k), vt, nb, da_lambda[l].astype(F32),
                         da_subln_g[l][:, None], lam_init)
        x2d = _merge_ffn(
            x2d, s, l, oda.reshape(b * s, DA_V_W), oret, osc, gates,
            w_branch_da, w_branch_ret, w_branch_sc, w_out, norm_ffn_g[l][None, :],
            w_ffn_in, ffn_conv_w[l], ffn_conv_b[l][None, :], w_ffn_out)
    return x2d.reshape(b, s, d)


def kernel(x, rel_bias, norm_mix_g, w_in, b_gate, da_q_norm_g, da_k_norm_g, da_lambda, da_subln_g,
           ret_norm_g, sc_conv_w, sc_conv_b, w_branch_da, w_branch_ret, w_branch_sc, w_out,
           norm_ffn_g, w_ffn_in, ffn_conv_w, ffn_conv_b, w_ffn_out):
    return _forward(x, rel_bias, norm_mix_g, w_in, b_gate, da_q_norm_g, da_k_norm_g, da_lambda,
                    da_subln_g, ret_norm_g, sc_conv_w, sc_conv_b, w_branch_da, w_branch_ret,
                    w_branch_sc, w_out, norm_ffn_g, w_ffn_in, ffn_conv_w, ffn_conv_b, w_ffn_out)
```

```python
import functools
import math

import jax
import jax.numpy as jnp
from jax import lax
from jax.experimental import pallas as pl
from jax.experimental.pallas import tpu as pltpu

D_MODEL = 1024
DA_HEADS = 4
DA_QK_DIM = 64
DA_V_DIM = 2 * DA_QK_DIM
RET_HEADS = 4
RET_QK_DIM = 64
RET_V_DIM = 64
SC_WIDTH = 256
CONV_W = 3
D_FF = 2816
REL_BUCKETS = 32
REL_MAX_DIST = 128
ROPE_THETA = 10000.0
EPS = 1e-6
N_BRANCH = 3

DA_QK_W = DA_HEADS * 2 * DA_QK_DIM
DA_V_W = DA_HEADS * DA_V_DIM
RET_QK_W = RET_HEADS * RET_QK_DIM
RET_V_W = RET_HEADS * RET_V_DIM

OFF_DA_Q = 0
OFF_DA_K = OFF_DA_Q + DA_QK_W
OFF_DA_V = OFF_DA_K + DA_QK_W
OFF_R_Q = OFF_DA_V + DA_V_W
OFF_R_K = OFF_R_Q + RET_QK_W
OFF_R_V = OFF_R_K + RET_QK_W
OFF_R_G = OFF_R_V + RET_V_W
OFF_SC_B = OFF_R_G + RET_V_W
OFF_SC_C = OFF_SC_B + SC_WIDTH
OFF_SC_X = OFF_SC_C + SC_WIDTH
OFF_GATE = OFF_SC_X + SC_WIDTH
IN_WIDTH = OFF_GATE + N_BRANCH * D_MODEL

LANES = 128
MXU_WIDTH = 256
NORM_GROUP = 64
SUBLANES = 8
ROW_BLOCK = 512
ATT_BLOCK = 512
RET_CHUNK = 128
GATE_CHUNK = 512
FF_CHUNK = 256
N_FF_CHUNKS = D_FF // FF_CHUNK
VMEM_LIMIT = 56 * 1024 * 1024
NEG = -1e30
LOG2E = math.log2(math.e)
BF16 = jnp.bfloat16
F32 = jnp.float32


def _resident(shape):
    nd = len(shape)
    return pl.BlockSpec(shape, lambda *_: (0,) * nd, pipeline_mode=pl.Buffered(1))


def _resident_layer(stacked_shape, layer):
    _, rows, cols = stacked_shape
    return pl.BlockSpec((None, rows, cols), lambda *_: (layer, 0, 0), pipeline_mode=pl.Buffered(1))


def _group_mean_sq(y, bd):
    return jnp.dot((y * y).astype(BF16), bd, preferred_element_type=F32) * (1.0 / NORM_GROUP)


def _sigmoid(z):
    return 1.0 / (1.0 + jnp.exp(-z))


def _in_proj_kernel(x_ref, g_ref, w_ref, bg_ref, qg_ref, kg_ref, cos_ref, sina_ref, sinb_ref,
                    cw_ref, cb_ref, bd_ref, rn_ref,
                    q_out, k_out, vt_out, ret_out, sc_out, gate_out,
                    h_ref, ext_ref, state_ref, *, blocks_per_seq):
    tm = x_ref.shape[0]

    @pl.when(pl.program_id(0) % blocks_per_seq == 0)
    def _():
        ext_ref[0:SUBLANES, :] = jnp.zeros((SUBLANES, SC_WIDTH), F32)
        state_ref[...] = jnp.zeros_like(state_ref)

    x = x_ref[...]
    ms = jnp.mean(x * x, axis=-1, keepdims=True)
    h_ref[...] = (x * lax.rsqrt(ms + EPS) * g_ref[...]).astype(BF16)
    bd = bd_ref[...]

    def proj(off, width):
        return jnp.dot(h_ref[...], w_ref[:, off:off + width], preferred_element_type=F32)

    halves = range(DA_QK_W // MXU_WIDTH)
    yq = [proj(OFF_DA_Q + half * MXU_WIDTH, MXU_WIDTH) for half in halves]
    yk = [proj(OFF_DA_K + half * MXU_WIDTH, MXU_WIDTH) for half in halves]
    v = proj(OFF_DA_V, DA_V_W)
    for c in range(tm // ATT_BLOCK):
        vt_out[0, c] = v[c * ATT_BLOCK:(c + 1) * ATT_BLOCK, :].T.astype(BF16)

    def rotary(y):
        outs = []
        for c in range(RET_QK_W // LANES):
            yc = y[:, c * LANES:(c + 1) * LANES]
            outs.append(yc * cos_ref[...]
                        + pltpu.roll(yc, LANES - 32, axis=1) * sina_ref[...]
                        + pltpu.roll(yc, 32, axis=1) * sinb_ref[...])
        return jnp.concatenate(outs, axis=1)

    rq = rotary(proj(OFF_R_Q, RET_QK_W)).astype(BF16)
    rk = (rotary(proj(OFF_R_K, RET_QK_W)) * (RET_QK_DIM ** -0.5)).astype(BF16)

    for half in halves:
        sl = slice(half * MXU_WIDTH, (half + 1) * MXU_WIDTH)
        q_out[:, sl] = (yq[half] * lax.rsqrt(_group_mean_sq(yq[half], bd) + EPS)
                        * (qg_ref[...] * (DA_QK_DIM ** -0.5 * LOG2E))).astype(BF16)
        k_out[:, sl] = (yk[half] * lax.rsqrt(_group_mean_sq(yk[half], bd) + EPS)
                        * kg_ref[...]).astype(BF16)

    rv = proj(OFF_R_V, RET_V_W).astype(BF16)
    ret_steps = _retention_steps(rq, rk, rv, proj(OFF_R_G, RET_V_W), rn_ref[...], bd, state_ref,
                                 ret_out)

    next(ret_steps, None)
    u = proj(OFF_SC_C, SC_WIDTH) * proj(OFF_SC_X, SC_WIDTH)
    ext_ref[SUBLANES:SUBLANES + tm, :] = u
    conv = (cb_ref[...]
            + cw_ref[0:1, :] * ext_ref[SUBLANES - 2:SUBLANES - 2 + tm, :]
            + cw_ref[1:2, :] * ext_ref[SUBLANES - 1:SUBLANES - 1 + tm, :]
            + cw_ref[2:3, :] * u)
    ext_ref[0:SUBLANES, :] = u[tm - SUBLANES:, :]
    next(ret_steps, None)
    sc_out[...] = (proj(OFF_SC_B, SC_WIDTH) * conv).astype(BF16)

    for c in range(N_BRANCH * D_MODEL // GATE_CHUNK):
        next(ret_steps, None)
        sl = slice(c * GATE_CHUNK, (c + 1) * GATE_CHUNK)
        z = proj(OFF_GATE + c * GATE_CHUNK, GATE_CHUNK) + bg_ref[:, sl]
        gate_out[:, sl] = _sigmoid(z).astype(BF16)
    for _ in ret_steps:
        pass


def _in_proj(x2d, seq, layer, g, w, bg, qg, kg, cos_t, sina_t, sinb_t, cw, cb, bd, rn):
    t = x2d.shape[0]
    tm = ROW_BLOCK
    blocks_per_seq = seq // tm
    row = lambda width: pl.BlockSpec((tm, width), lambda i: (i, 0))
    tab = pl.BlockSpec((tm, LANES), lambda i: (i % blocks_per_seq, 0))
    out_widths = (DA_QK_W, DA_QK_W, RET_V_W, SC_WIDTH, N_BRANCH * D_MODEL)
    out_specs = [row(wd) for wd in out_widths]
    out_shape = [jax.ShapeDtypeStruct((t, wd), BF16) for wd in out_widths]
    out_specs.insert(2, pl.BlockSpec((1, tm // ATT_BLOCK, DA_V_W, ATT_BLOCK),
                                     lambda i: (i // blocks_per_seq, i % blocks_per_seq, 0, 0)))
    out_shape.insert(2, jax.ShapeDtypeStruct((t // seq, seq // ATT_BLOCK, DA_V_W, ATT_BLOCK), BF16))
    return pl.pallas_call(
        functools.partial(_in_proj_kernel, blocks_per_seq=blocks_per_seq),
        grid=(t // tm,),
        in_specs=[row(D_MODEL), _resident(g.shape), _resident_layer(w.shape, layer),
                  _resident(bg.shape),
                  _resident(qg.shape), _resident(kg.shape), tab, tab, tab,
                  _resident(cw.shape), _resident(cb.shape), _resident(bd.shape),
                  _resident(rn.shape)],
        out_specs=out_specs,
        out_shape=out_shape,
        scratch_shapes=[pltpu.VMEM((tm, D_MODEL), BF16),
                        pltpu.VMEM((tm + SUBLANES, SC_WIDTH), F32),
                        pltpu.VMEM((RET_QK_W, RET_V_W), F32)],
        compiler_params=pltpu.CompilerParams(dimension_semantics=("arbitrary",),
                                             vmem_limit_bytes=VMEM_LIMIT),
        name="in_proj",
    )(x2d, g, w, bg, qg, kg, cos_t, sina_t, sinb_t, cw, cb, bd, rn)


def _diff_attn_kernel(q_ref, k_ref, vt_ref, nb_ref, lam_ref, sg_ref, o_ref,
                      s0_ref, s1_ref, acc0_ref, acc1_ref, *, lam_init):
    tq = o_ref.shape[1]
    qi = pl.program_id(2)
    n_q = pl.num_programs(2)

    def masked_q(block):
        q = q_ref[0, pl.ds(pl.multiple_of(block * tq, tq), tq), :]
        lane = lax.broadcasted_iota(jnp.int32, q.shape, 1)
        zero = jnp.zeros_like(q)
        return jnp.where(lane < DA_QK_DIM, q, zero), jnp.where(lane >= DA_QK_DIM, q, zero)

    q_map = masked_q(qi)
    s_refs = (s0_ref, s1_ref)
    acc_refs = (acc0_ref, acc1_ref)

    def put_scores(mp, j, far=False, q_block=qi, q_rows=None):
        start = pl.multiple_of(j * tq, tq)
        q_rows = q_map[mp] if q_rows is None else q_rows
        s = lax.dot_general(k_ref[0, pl.ds(start, tq), :], q_rows, (((1,), (1,)), ((), ())),
                            preferred_element_type=F32)
        s_refs[mp][...] = s if far else s + nb_ref[0, jnp.minimum(q_block - j, 2)]

    def update(mp, j, stats):
        m, l = stats
        m_new = jnp.maximum(m, jnp.max(s_refs[mp][...], axis=0, keepdims=True))
        alpha = jnp.exp2(m - m_new)
        p = jnp.exp2(s_refs[mp][...] - m_new)
        l = alpha * l + jnp.sum(p, axis=0, keepdims=True)
        acc_refs[mp][...] = alpha * acc_refs[mp][...] + jnp.dot(
            vt_ref[0, j], p.astype(BF16), preferred_element_type=F32)
        return m_new, l

    def step(j, stats, far=False):
        put_scores(1, j, far)
        stats0 = update(0, j, stats[0])
        put_scores(0, j + 1, far)
        stats1 = update(1, j, stats[1])
        return stats0, stats1

    init = (jnp.full((1, tq), NEG, F32), jnp.zeros((1, tq), F32))
    acc0_ref[...] = jnp.zeros_like(acc0_ref)
    acc1_ref[...] = jnp.zeros_like(acc1_ref)

    @pl.when(qi == 0)
    def _():
        put_scores(0, 0)

    n_far = jnp.maximum(qi - 2, 0)

    def far_pair(t, stats):
        return step(2 * t + 1, step(2 * t, stats, far=True), far=True)

    stats = lax.fori_loop(0, lax.shift_right_logical(n_far, 1), far_pair, (init, init))
    def finish(stats):
        put_scores(1, qi)
        _, l0 = update(0, qi, stats[0])
        nxt = jnp.minimum(qi + 1, n_q - 1)
        put_scores(0, 0, q_block=nxt, q_rows=masked_q(nxt)[0])
        _, l1 = update(1, qi, stats[1])

        lp = lam_ref[...]
        lam = (jnp.exp(jnp.sum(lp[0:1] * lp[1:2], axis=1, keepdims=True))
               - jnp.exp(jnp.sum(lp[2:3] * lp[3:4], axis=1, keepdims=True)) + lam_init)
        o = acc0_ref[...] * (1.0 / l0) - lam * (acc1_ref[...] * (1.0 / l1))
        ms = jnp.mean(o * o, axis=0, keepdims=True)
        o_ref[0] = (o * lax.rsqrt(ms + EPS) * (sg_ref[...] * (1.0 - lam_init))).T.astype(BF16)
        return 0

    def last_steps(st):
        return finish(step(qi - 1, step(qi - 2, st)))

    lax.cond(
        qi >= 2,
        lambda st: lax.cond((n_far & 1) == 1,
                            lambda s1: last_steps(step(n_far - 1, s1, far=True)), last_steps, st),
        lambda st: lax.cond(qi == 1, lambda s1: finish(step(0, s1)), finish, st), stats)


def _diff_attn(q, k, vt, nb, lam_p, sg, lam_init):
    b, s, _ = q.shape
    tq = ATT_BLOCK
    return pl.pallas_call(
        functools.partial(_diff_attn_kernel, lam_init=lam_init),
        grid=(b, DA_HEADS, s // tq),
        in_specs=[pl.BlockSpec((1, s, LANES), lambda bi, h, i: (bi, 0, h)),
                  pl.BlockSpec((1, s, LANES), lambda bi, h, i: (bi, 0, h)),
                  pl.BlockSpec((1, s // tq, DA_V_DIM, tq), lambda bi, h, i: (bi, 0, h, 0)),
                  pl.BlockSpec((1, 3, tq, tq), lambda bi, h, i: (h, 0, 0, 0)),
                  pl.BlockSpec(lam_p.shape, lambda bi, h, i: (0, 0)),
                  pl.BlockSpec(sg.shape, lambda bi, h, i: (0, 0))],
        out_specs=pl.BlockSpec((1, tq, LANES), lambda bi, h, i: (bi, i, h)),
        out_shape=jax.ShapeDtypeStruct((b, s, DA_V_W), BF16),
        scratch_shapes=[pltpu.VMEM((tq, tq), F32), pltpu.VMEM((tq, tq), F32),
                        pltpu.VMEM((DA_V_DIM, tq), F32), pltpu.VMEM((DA_V_DIM, tq), F32)],
        compiler_params=pltpu.CompilerParams(
            dimension_semantics=("arbitrary", "arbitrary", "arbitrary"),
            vmem_limit_bytes=VMEM_LIMIT),
        name="diff_attn",
    )(q, k, vt, nb, lam_p, sg)


def _ret_log_gamma(head):
    lg = [math.log(1.0 - 2.0 ** (-5.0 - h)) for h in range(RET_HEADS)]
    out = jnp.full(head.shape, lg[RET_HEADS - 1], F32)
    for h in range(RET_HEADS - 2, -1, -1):
        out = jnp.where(head == h, lg[h], out)
    return out


def _retention_steps(q, k, v, gate, ng, bd, state_ref, o_ref):
    c = RET_CHUNK
    w = RET_QK_W
    n_chunks = q.shape[0] // c
    head_shift = RET_QK_DIM.bit_length() - 1
    chunk_shift = c.bit_length() - 1
    assert 1 << head_shift == RET_QK_DIM == RET_V_DIM and 1 << chunk_shift == c
    lane_head = lax.broadcasted_iota(jnp.int32, (c, w), 1) >> head_shift
    row = lax.broadcasted_iota(jnp.int32, (c, w), 0).astype(F32)
    lg_lane = _ret_log_gamma(lane_head)
    q_decay = jnp.exp(lg_lane * (row + 1.0))
    k_decay = jnp.exp(lg_lane * (c - 1.0 - row))
    chunk_decay = jnp.exp(lg_lane[0:1, :] * float(c))
    srow = lax.broadcasted_iota(jnp.int32, (RET_HEADS * c, c), 0)
    scol = lax.broadcasted_iota(jnp.int32, (RET_HEADS * c, c), 1)
    diff = ((srow & (c - 1)) - scol).astype(F32)
    d_intra = jnp.where(diff >= 0,
                        jnp.exp(_ret_log_gamma(srow >> chunk_shift) * jnp.maximum(diff, 0.0)), 0.0)
    st_row_head = lax.broadcasted_iota(jnp.int32, (w, w), 0) >> head_shift
    st_col_head = lax.broadcasted_iota(jnp.int32, (w, w), 1) >> head_shift
    same_head = st_row_head == st_col_head

    for n in range(n_chunks):
        sl = slice(n * c, (n + 1) * c)
        qc, kc, vc = q[sl], k[sl], v[sl]
        zero = jnp.zeros_like(qc)
        qs = jnp.concatenate([jnp.where(lane_head == h, qc, zero) for h in range(RET_HEADS)], axis=0)
        s = lax.dot_general(qs, kc, (((1,), (1,)), ((), ())), preferred_element_type=F32)
        state = state_ref[...]
        cross = jnp.dot(qc, state.astype(BF16), preferred_element_type=F32) * q_decay
        k_dec = (kc.astype(F32) * k_decay).T.astype(BF16)
        kv = jnp.dot(k_dec, vc, preferred_element_type=F32)
        state_ref[...] = state * chunk_decay + jnp.where(same_head, kv, 0.0)
        yield
        o_all = jnp.dot((s * d_intra).astype(BF16), vc, preferred_element_type=F32)
        inner = jnp.zeros((c, w), F32)
        for h in range(RET_HEADS):
            inner = jnp.where(lane_head == h, o_all[h * c:(h + 1) * c], inner)
        o = inner + cross
        yield
        y = o * lax.rsqrt(_group_mean_sq(o, bd) + EPS) * ng
        g = gate[sl]
        o_ref[sl, :] = (y * (g * _sigmoid(g))).astype(BF16)


def _merge_ffn_kernel(x_ref, oda_ref, oret_ref, osc_ref, gate_ref, wda_ref, wret_ref, wsc_ref,
                      wout_ref, ng_ref, win_ref, cw_ref, cb_ref, wo_ref,
                      out_ref, h_ref, ext_ref, carry_ref, act_ref, *, blocks_per_seq):
    tm = x_ref.shape[0]
    d = D_MODEL
    y = (gate_ref[:, 0:d].astype(F32)
         * jnp.dot(oda_ref[...], wda_ref[...], preferred_element_type=F32))
    y += (gate_ref[:, d:2 * d].astype(F32)
          * jnp.dot(oret_ref[...], wret_ref[...], preferred_element_type=F32))
    y += (gate_ref[:, 2 * d:3 * d].astype(F32)
          * jnp.dot(osc_ref[...], wsc_ref[...], preferred_element_type=F32))
    x1 = x_ref[...] + jnp.dot(y.astype(BF16), wout_ref[...], preferred_element_type=F32)
    out_ref[...] = x1
    ms = jnp.mean(x1 * x1, axis=-1, keepdims=True)
    h_ref[...] = (x1 * lax.rsqrt(ms + EPS) * ng_ref[...]).astype(BF16)

    @pl.when(pl.program_id(0) % blocks_per_seq == 0)
    def _():
        carry_ref[...] = jnp.zeros_like(carry_ref)

    for c in range(N_FF_CHUNKS):
        sl = slice(c * FF_CHUNK, (c + 1) * FF_CHUNK)
        g = jnp.dot(h_ref[...], win_ref[:, sl], preferred_element_type=F32)
        u = jnp.dot(h_ref[...], win_ref[:, D_FF + c * FF_CHUNK:D_FF + (c + 1) * FF_CHUNK],
                    preferred_element_type=F32)
        ext_ref[0:SUBLANES, :] = carry_ref[:, sl]
        ext_ref[SUBLANES:SUBLANES + tm, :] = g
        gc = (cb_ref[:, sl]
              + cw_ref[0:1, sl] * ext_ref[SUBLANES - 2:SUBLANES - 2 + tm, :]
              + cw_ref[1:2, sl] * ext_ref[SUBLANES - 1:SUBLANES - 1 + tm, :]
              + cw_ref[2:3, sl] * g)
        carry_ref[:, sl] = g[tm - SUBLANES:, :]
        act_ref[:, sl] = (gc * _sigmoid(gc) * u).astype(BF16)

    out_ref[...] += jnp.dot(act_ref[...], wo_ref[...], preferred_element_type=F32)


def _merge_ffn(x2d, seq, layer, oda, oret, osc, gates, wda, wret, wsc, wout, ng, win, cw, cb, wo):
    t = x2d.shape[0]
    tm = ROW_BLOCK
    blocks_per_seq = seq // tm
    row = lambda width: pl.BlockSpec((tm, width), lambda i: (i, 0))
    return pl.pallas_call(
        functools.partial(_merge_ffn_kernel, blocks_per_seq=blocks_per_seq),
        grid=(t // tm,),
        in_specs=[row(D_MODEL), row(DA_V_W), row(RET_V_W), row(SC_WIDTH), row(N_BRANCH * D_MODEL)]
                 + [_resident_layer(a.shape, layer) if a.ndim == 3 else _resident(a.shape)
                    for a in (wda, wret, wsc, wout, ng, win, cw, cb, wo)],
        out_specs=row(D_MODEL),
        out_shape=jax.ShapeDtypeStruct((t, D_MODEL), F32),
        scratch_shapes=[pltpu.VMEM((tm, D_MODEL), BF16),
                        pltpu.VMEM((tm + SUBLANES, FF_CHUNK), F32),
                        pltpu.VMEM((SUBLANES, D_FF), F32),
                        pltpu.VMEM((tm, D_FF), BF16)],
        compiler_params=pltpu.CompilerParams(dimension_semantics=("arbitrary",),
                                             vmem_limit_bytes=VMEM_LIMIT),
        name="merge_ffn",
    )(x2d, oda, oret, osc, gates, wda, wret, wsc, wout, ng, win, cw, cb, wo)


def _rel_bucket(n):
    max_exact = REL_BUCKETS // 2
    nf = jnp.maximum(n, 1).astype(F32)
    large = max_exact + (jnp.log(nf / max_exact) / math.log(REL_MAX_DIST / max_exact)
                         * (REL_BUCKETS - max_exact)).astype(jnp.int32)
    large = jnp.minimum(large, REL_BUCKETS - 1)
    return jnp.where(n < max_exact, n, large)


def _near_bias_tiles(rel_bias, tq):
    hb = tq // 2
    assert tq == 2 * hb and hb >= REL_MAX_DIST
    n_heads = rel_bias.shape[1]
    dist = jnp.arange(2 * hb)
    shifted = (rel_bias[_rel_bucket(dist)] - rel_bias[REL_BUCKETS - 1]) * LOG2E
    by_offset = jnp.concatenate([jnp.full((hb - 1, n_heads), NEG, F32), shifted,
                                 jnp.zeros((1, n_heads), F32)], axis=0).T
    m = by_offset.shape[1]
    skew = jnp.tile(by_offset, (1, hb))[:, :hb * (m - 1)].reshape(n_heads, hb, m - 1)
    near = skew[:, :, hb - 1:2 * hb - 1]
    next_ = skew[:, :, 2 * hb - 1:3 * hb - 1]
    zero = jnp.zeros_like(near)
    masked = jnp.full_like(near, NEG)
    grid2 = lambda rows: jnp.concatenate([jnp.concatenate(r, axis=-1) for r in rows], axis=-2)
    diag = grid2([[near, next_], [masked, near]])
    prev = grid2([[zero, zero], [next_, zero]])
    return jnp.stack([diag, prev, jnp.zeros_like(diag)], axis=1).astype(F32)


def _rotary_tables(seq):
    half = RET_QK_DIM // 2
    inv = ROPE_THETA ** (-jnp.arange(half, dtype=F32) / half)
    ang = jnp.arange(seq).astype(F32)[:, None] * inv[None, :]
    cos, sin = jnp.cos(ang), jnp.sin(ang)
    zero = jnp.zeros_like(sin)
    reps = LANES // RET_QK_DIM
    cos_t = jnp.tile(jnp.concatenate([cos, cos], axis=1), (1, reps))
    sina_t = jnp.tile(jnp.concatenate([-sin, zero], axis=1), (1, reps))
    sinb_t = jnp.tile(jnp.concatenate([zero, sin], axis=1), (1, reps))
    return cos_t, sina_t, sinb_t


@jax.jit
def _forward(x, rel_bias, norm_mix_g, w_in, b_gate, da_q_norm_g, da_k_norm_g, da_lambda,
             da_subln_g, ret_norm_g, sc_conv_w, sc_conv_b, w_branch_da, w_branch_ret,
             w_branch_sc, w_out, norm_ffn_g, w_ffn_in, ffn_conv_w, ffn_conv_b, w_ffn_out):
    b, s, d = x.shape
    depth = w_in.shape[0]
    assert d == D_MODEL and s % ROW_BLOCK == 0 and s % ATT_BLOCK == 0
    nb = _near_bias_tiles(rel_bias.astype(F32), ATT_BLOCK)
    cos_t, sina_t, sinb_t = _rotary_tables(s)
    assert NORM_GROUP == DA_QK_DIM == RET_V_DIM and RET_V_W == MXU_WIDTH
    group = jnp.arange(MXU_WIDTH) // NORM_GROUP
    bd = (group[:, None] == group[None, :]).astype(BF16)

    w_in, w_branch_da, w_branch_ret, w_branch_sc, w_out, w_ffn_in, w_ffn_out = (
        w.astype(BF16) for w in (w_in, w_branch_da, w_branch_ret, w_branch_sc, w_out, w_ffn_in,
                                 w_ffn_out))
    x2d = x.reshape(b * s, d)
    for l in range(depth):
        lam_init = 0.8 - 0.6 * math.exp(-0.3 * l)
        q, k, vt, oret, osc, gates = _in_proj(
            x2d, s, l, norm_mix_g[l][None, :], w_in, b_gate[l][None, :],
            jnp.tile(da_q_norm_g[l], MXU_WIDTH // DA_QK_DIM)[None, :],
            jnp.tile(da_k_norm_g[l], MXU_WIDTH // DA_QK_DIM)[None, :],
            cos_t, sina_t, sinb_t, sc_conv_w[l], sc_conv_b[l][None, :], bd,
            jnp.tile(ret_norm_g[l], RET_HEADS)[None, :])
        to3 = lambda a: a.reshape(b, s, a.shape[-1])
        oda = _diff_attn(to3(q), to3(k), vt, nb, da_lambda[l].astype(F32),
                         da_subln_g[l][:, None], lam_init)
        x2d = _merge_ffn(
            x2d, s, l, oda.reshape(b * s, DA_V_W), oret, osc, gates,
            w_branch_da, w_branch_ret, w_branch_sc, w_out, norm_ffn_g[l][None, :],
            w_ffn_in, ffn_conv_w[l], ffn_conv_b[l][None, :], w_ffn_out)
    return x2d.reshape(b, s, d)


def kernel(x, rel_bias, norm_mix_g, w_in, b_gate, da_q_norm_g, da_k_norm_g, da_lambda, da_subln_g,
           ret_norm_g, sc_conv_w, sc_conv_b, w_branch_da, w_branch_ret, w_branch_sc, w_out,
           norm_ffn_g, w_ffn_in, ffn_conv_w, ffn_conv_b, w_ffn_out):
    return _forward(x, rel_bias, norm_mix_g, w_in, b_gate, da_q_norm_g, da_k_norm_g, da_lambda,
                    da_subln_g, ret_norm_g, sc_conv_w, sc_conv_b, w_branch_da, w_branch_ret,
                    w_branch_sc, w_out, norm_ffn_g, w_ffn_in, ffn_conv_w, ffn_conv_b, w_ffn_out)
```
